```python
import math
import jax
import jax.numpy as jnp
from jax import lax
import numpy as np

D_MODEL = 2048
BATCH = 2
SEQ = 8192
DEPTH = 4

GRID_W = 64
CTX_LEN = 256
N_MIXERS = 3
ALPHA = (2.0 * DEPTH) ** 0.25
BETA = (8.0 * DEPTH) ** -0.25
LN_EPS = 1e-6
Q_BLOCK = 128
ROPE_DIM = 64
ROPE_BASE = 10000.0
GM_CHUNK = 128
GM_DIM = D_MODEL
GM_GROUPS = GM_DIM // 128
GM_GDIM = GM_DIM // GM_GROUPS
MLA_HEADS = D_MODEL // 128
MLA_Q_RANK = D_MODEL // 4
MLA_KV_RANK = D_MODEL // 4
MLA_NOPE = 128
MLA_ROPE = ROPE_DIM
MLA_V = 128
MLA_SCALE = (MLA_NOPE + MLA_ROPE) ** -0.5
DA_HEAD = ROPE_DIM
DA_HEADS = D_MODEL // (2 * DA_HEAD)
DA_SCALE = DA_HEAD ** -0.5
PEER_HEADS = 8
PEER_TOPK = 16
PEER_NKEYS = 128
PEER_EXPERTS = PEER_NKEYS * PEER_NKEYS
PEER_DK = 256
PEER_CHUNK = 128
N_LAYERS_A = len(range(0, DEPTH, N_MIXERS))
N_LAYERS_B = len(range(1, DEPTH, N_MIXERS))
N_LAYERS_C = len(range(2, DEPTH, N_MIXERS))

kernel_name = 'hybrid_dit_gmlp_mla_diffattn_peer'


def _layer_norm(x):
    xf = x.astype(jnp.float32)
    mu = jnp.mean(xf, axis=-1, keepdims=True)
    var = jnp.mean(jnp.square(xf - mu), axis=-1, keepdims=True)
    return ((xf - mu) * lax.rsqrt(var + LN_EPS)).astype(x.dtype)


def _rms_norm(x):
    xf = x.astype(jnp.float32)
    return (xf * lax.rsqrt(jnp.mean(jnp.square(xf), axis=-1, keepdims=True) + LN_EPS)).astype(x.dtype)


def _modulate(x, shift, scale):
    return _layer_norm(x) * (1.0 + scale) + shift


def _post_norm(x, y, g, b):
    return _layer_norm(ALPHA * x + y) * g + b


def axial_rope_tables(n_tokens):
    rows = n_tokens // GRID_W
    row = jnp.repeat(jnp.arange(rows, dtype=jnp.float32), GRID_W)
    col = jnp.tile(jnp.arange(GRID_W, dtype=jnp.float32), rows)
    nf = ROPE_DIM // 4
    inv = ROPE_BASE ** (-jnp.arange(nf, dtype=jnp.float32) / nf)
    ang_r = row[:, None] * inv
    ang_c = col[:, None] * inv
    return (jnp.cos(ang_r), jnp.sin(ang_r), jnp.cos(ang_c), jnp.sin(ang_c))


def _rope_half(x, cos, sin):
    x1, x2 = jnp.split(x, 2, axis=-1)
    return jnp.concatenate([x1 * cos - x2 * sin, x1 * sin + x2 * cos], axis=-1)


def axial_rope(x, rope):
    cos_r, sin_r, cos_c, sin_c = rope
    xf = x.astype(jnp.float32)
    xr, xc = jnp.split(xf, 2, axis=-1)
    out = jnp.concatenate([_rope_half(xr, cos_r, sin_r), _rope_half(xc, cos_c, sin_c)], axis=-1)
    return out.astype(x.dtype)


def _merge_heads(o):
    B, H, T, dv = o.shape
    return o.transpose(0, 2, 1, 3).reshape(B, T, H * dv)


def _over_query_blocks(fn, qs):
    B, H, S, _ = qs[0].shape
    nb = S // Q_BLOCK
    blocks = tuple(q.reshape(B, H, nb, Q_BLOCK, q.shape[-1]).transpose(2, 0, 1, 3, 4) for q in qs)
    o = lax.map(fn, blocks)
    return o.transpose(1, 2, 0, 3, 4).reshape(B, H, S, o.shape[-1])


def softmax_attention(q, k, v, scale):
    def blk(qs):
        (qb,) = qs
        s = jnp.einsum('bhqd,bhkd->bhqk', qb, k) * scale
        p = jax.nn.softmax(s.astype(jnp.float32), axis=-1)
        return jnp.einsum('bhqk,bhkd->bhqd', p.astype(v.dtype), v)
    return _over_query_blocks(blk, (q,))


def diff_attention(q1, q2, k1, k2, v, lam, scale):
    def blk(qs):
        a, b = qs
        p1 = jax.nn.softmax((jnp.einsum('bhqd,bhkd->bhqk', a, k1) * scale).astype(jnp.float32), axis=-1)
        p2 = jax.nn.softmax((jnp.einsum('bhqd,bhkd->bhqk', b, k2) * scale).astype(jnp.float32), axis=-1)
        return jnp.einsum('bhqk,bhkd->bhqd', (p1 - lam * p2).astype(v.dtype), v)
    return _over_query_blocks(blk, (q1, q2))


def gmlp_mixer(hl, hc, w_in, ln_g, ln_b, ws, bs, w_out):
    def mix(h):
        B, T, _ = h.shape
        z = jax.nn.gelu(h @ w_in)
        u, v = jnp.split(z, 2, axis=-1)
        v = _layer_norm(v) * ln_g + ln_b
        v = v.reshape(B, T // GM_CHUNK, GM_CHUNK, GM_GROUPS, GM_GDIM)
        sv = jnp.einsum('gpq,bnqgd->bnpgd', ws, v) + bs.T[:, :, None]
        return (u * sv.reshape(B, T, GM_DIM)) @ w_out
    return mix(hl), (mix(hc) if hc is not None else None)


def mla_mixer(hl, hc, rope, w_in, q_norm, kv_norm, w_uq, w_ukv, w_out, ctx_out):
    def down(h):
        return jnp.split(h @ w_in, [MLA_Q_RANK, MLA_Q_RANK + MLA_KV_RANK], axis=-1)

    def up_q(cq):
        B, T, _ = cq.shape
        q = (_rms_norm(cq) * q_norm) @ w_uq
        q = q.reshape(B, T, MLA_HEADS, MLA_NOPE + MLA_ROPE).transpose(0, 2, 1, 3)
        return jnp.split(q, [MLA_NOPE], axis=-1)

    def up_kv(ckv, k_rope):
        B, T, _ = ckv.shape
        kv = (_rms_norm(ckv) * kv_norm) @ w_ukv
        kv = kv.reshape(B, T, MLA_HEADS, MLA_NOPE + MLA_V).transpose(0, 2, 1, 3)
        k_nope, v = jnp.split(kv, [MLA_NOPE], axis=-1)
        k_rope = jnp.broadcast_to(k_rope, k_nope.shape[:-1] + (MLA_ROPE,))
        return jnp.concatenate([k_nope, k_rope], axis=-1), v

    cq_l, ckv_l, kr_l = down(hl)
    cq_c, ckv_c, kr_c = down(hc)
    qn_l, qr_l = up_q(cq_l)
    q_l = jnp.concatenate([qn_l, axial_rope(qr_l, rope)], axis=-1)
    k_l, v_l = up_kv(ckv_l, axial_rope(kr_l[:, None], rope))
    k_c, v_c = up_kv(ckv_c, kr_c[:, None])
    o_l = softmax_attention(q_l, jnp.concatenate([k_c, k_l], axis=2), jnp.concatenate([v_c, v_l], axis=2), MLA_SCALE)
    y_l = _merge_heads(o_l) @ w_out
    if ctx_out:
        q_c = jnp.concatenate(up_q(cq_c), axis=-1)
        y_c = _merge_heads(softmax_attention(q_c, k_c, v_c, MLA_SCALE)) @ w_out
    else:
        y_c = None
    return y_l, y_c


def diff_attn_mixer(hl, hc, rope, w_in, lam_p, subln, w_out, lam_init, ctx_out):
    def proj(h):
        B, T, _ = h.shape
        q, k, v = jnp.split(h @ w_in, 3, axis=-1)
        q = q.reshape(B, T, DA_HEADS, 2, DA_HEAD).transpose(3, 0, 2, 1, 4)
        k = k.reshape(B, T, DA_HEADS, 2, DA_HEAD).transpose(3, 0, 2, 1, 4)
        v = v.reshape(B, T, DA_HEADS, 2 * DA_HEAD).transpose(0, 2, 1, 3)
        return q, k, v

    lp = lam_p.astype(jnp.float32)
    lam = jnp.exp(jnp.sum(lp[0] * lp[1])) - jnp.exp(jnp.sum(lp[2] * lp[3])) + lam_init

    def out(o):
        o = _rms_norm(o) * subln * (1.0 - lam_init)
        return _merge_heads(o) @ w_out

    q_l, k_l, v_l = proj(hl)
    q_l = axial_rope(q_l, rope)
    k_l = axial_rope(k_l, rope)
    q_c, k_c, v_c = proj(hc)
    k_all = jnp.concatenate([k_c, k_l], axis=3)
    v_all = jnp.concatenate([v_c, v_l], axis=2)
    y_l = out(diff_attention(q_l[0], q_l[1], k_all[0], k_all[1], v_all, lam, DA_SCALE))
    y_c = out(diff_attention(q_c[0], q_c[1], k_c[0], k_c[1], v_c, lam, DA_SCALE)) if ctx_out else None
    return y_l, y_c


def peer_ffn(h, wq, k1, k2, u_tab, v_tab):
    B, T, D = h.shape
    q = (h @ wq).reshape(B, T, PEER_HEADS, 2, PEER_DK // 2)
    s1 = jnp.einsum('bthd,nd->bthn', q[..., 0, :], k1)
    s2 = jnp.einsum('bthd,nd->bthn', q[..., 1, :], k2)
    v1, i1 = lax.top_k(s1, PEER_TOPK)
    v2, i2 = lax.top_k(s2, PEER_TOPK)
    cand = (v1[..., :, None] + v2[..., None, :]).reshape(B, T, PEER_HEADS, PEER_TOPK * PEER_TOPK)
    cidx = (i1[..., :, None] * PEER_NKEYS + i2[..., None, :]).reshape(B, T, PEER_HEADS, PEER_TOPK * PEER_TOPK)
    best, sel = lax.top_k(cand, PEER_TOPK)
    eidx = jnp.take_along_axis(cidx, sel, axis=-1)
    gate = jax.nn.softmax(best.astype(jnp.float32), axis=-1).astype(h.dtype)
    n = (B * T) // PEER_CHUNK
    kk = PEER_HEADS * PEER_TOPK

    def chunk(args):
        hc, ec, gc = args
        act = jax.nn.gelu(jnp.einsum('ckd,cd->ck', u_tab[ec], hc))
        return jnp.einsum('ck,ckd->cd', gc * act, v_tab[ec])

    y = lax.map(chunk, (h.reshape(n, PEER_CHUNK, D), eidx.reshape(n, PEER_CHUNK, kk), gate.reshape(n, PEER_CHUNK, kk)))
    return y.reshape(B, T, D)


def setup_inputs(seed: int = 0) -> dict:
    key = jax.random.key(seed)
    keys = iter(jax.random.split(key, 32))

    def nrm(shape, std):
        return jax.random.normal(next(keys), shape, jnp.float32) * std

    D = D_MODEL
    return {
        'x': nrm((BATCH, SEQ, D), 1.0),
        'c': nrm((BATCH, D), 1.0),
        'ctx': nrm((BATCH, CTX_LEN, D), 1.0),
        'c_ctx': nrm((D,), 1.0),
        'mod_w': nrm((DEPTH, D, 6 * D), 0.5 * D ** -0.5),
        'mod_b': nrm((DEPTH, 6 * D), 0.02),
        'ln_g': 1.0 + nrm((DEPTH, 2, D), 0.02),
        'ln_b': nrm((DEPTH, 2, D), 0.02),
        'peer_wq': nrm((DEPTH, D, PEER_HEADS * PEER_DK), D ** -0.5),
        'peer_k1': nrm((DEPTH, PEER_NKEYS, PEER_DK // 2), (PEER_DK // 2) ** -0.5),
        'peer_k2': nrm((DEPTH, PEER_NKEYS, PEER_DK // 2), (PEER_DK // 2) ** -0.5),
        'peer_u': nrm((DEPTH, PEER_EXPERTS, D), D ** -0.5),
        'peer_v': nrm((DEPTH, PEER_EXPERTS, D), BETA),
        'gm_w_in': nrm((N_LAYERS_A, D, 2 * GM_DIM), D ** -0.5),
        'gm_ln_g': 1.0 + nrm((N_LAYERS_A, GM_DIM), 0.02),
        'gm_ln_b': nrm((N_LAYERS_A, GM_DIM), 0.02),
        'gm_ws': nrm((N_LAYERS_A, GM_GROUPS, GM_CHUNK, GM_CHUNK), GM_CHUNK ** -0.5),
        'gm_bs': 1.0 + nrm((N_LAYERS_A, GM_GROUPS, GM_CHUNK), 0.1),
        'gm_w_out': nrm((N_LAYERS_A, GM_DIM, D), BETA * GM_DIM ** -0.5),
        'mla_w_in': nrm((N_LAYERS_B, D, MLA_Q_RANK + MLA_KV_RANK + MLA_ROPE), D ** -0.5),
        'mla_q_norm': 1.0 + nrm((N_LAYERS_B, MLA_Q_RANK), 0.02),
        'mla_kv_norm': 1.0 + nrm((N_LAYERS_B, MLA_KV_RANK), 0.02),
        'mla_w_uq': nrm((N_LAYERS_B, MLA_Q_RANK, MLA_HEADS * (MLA_NOPE + MLA_ROPE)), MLA_Q_RANK ** -0.5),
        'mla_w_ukv': nrm((N_LAYERS_B, MLA_KV_RANK, MLA_HEADS * (MLA_NOPE + MLA_V)), MLA_KV_RANK ** -0.5),
        'mla_w_out': nrm((N_LAYERS_B, MLA_HEADS * MLA_V, D), BETA * (MLA_HEADS * MLA_V) ** -0.5),
        'da_w_in': nrm((N_LAYERS_C, D, 3 * D), D ** -0.5),
        'da_lambda': nrm((N_LAYERS_C, 4, DA_HEAD), 0.1),
        'da_subln': 1.0 + nrm((N_LAYERS_C, 2 * DA_HEAD), 0.02),
        'da_w_out': nrm((N_LAYERS_C, D, D), BETA * D ** -0.5),
    }


def reference(x, c, ctx, c_ctx, mod_w, mod_b, ln_g, ln_b,
              peer_wq, peer_k1, peer_k2, peer_u, peer_v,
              gm_w_in, gm_ln_g, gm_ln_b, gm_ws, gm_bs, gm_w_out,
              mla_w_in, mla_q_norm, mla_kv_norm, mla_w_uq, mla_w_ukv, mla_w_out,
              da_w_in, da_lambda, da_subln, da_w_out):
    n_tokens = x.shape[1]
    rope = axial_rope_tables(n_tokens)
    silu_c = jax.nn.silu(c)
    silu_cc = jax.nn.silu(c_ctx)
    xl, xc = x, ctx
    for i in range(DEPTH):
        kind = i % N_MIXERS
        j = i // N_MIXERS
        last = i == DEPTH - 1
        mod_l = silu_c @ mod_w[i] + mod_b[i]
        mod_c = silu_cc @ mod_w[i] + mod_b[i]
        sh1, sc1, g1, sh2, sc2, g2 = jnp.split(mod_l[:, None, :], 6, axis=-1)
        csh1, csc1, cg1, csh2, csc2, cg2 = jnp.split(mod_c, 6, axis=-1)
        hl = _modulate(xl, sh1, sc1)
        need_ctx_in = (not last) or kind != 0
        hc = _modulate(xc, csh1, csc1) if need_ctx_in else None
        if kind == 0:
            yl, yc = gmlp_mixer(hl, hc, gm_w_in[j], gm_ln_g[j], gm_ln_b[j], gm_ws[j], gm_bs[j], gm_w_out[j])
        elif kind == 1:
            yl, yc = mla_mixer(hl, hc, rope, mla_w_in[j], mla_q_norm[j], mla_kv_norm[j],
                               mla_w_uq[j], mla_w_ukv[j], mla_w_out[j], not last)
        else:
            lam_init = 0.8 - 0.6 * math.exp(-0.3 * i)
            yl, yc = diff_attn_mixer(hl, hc, rope, da_w_in[j], da_lambda[j], da_subln[j],
                                     da_w_out[j], lam_init, not last)
        xl = _post_norm(xl, g1 * yl, ln_g[i, 0], ln_b[i, 0])
        yl = peer_ffn(_modulate(xl, sh2, sc2), peer_wq[i], peer_k1[i], peer_k2[i], peer_u[i], peer_v[i])
        xl = _post_norm(xl, g2 * yl, ln_g[i, 1], ln_b[i, 1])
        if not last:
            xc = _post_norm(xc, cg1 * yc, ln_g[i, 0], ln_b[i, 0])
            yc = peer_ffn(_modulate(xc, csh2, csc2), peer_wq[i], peer_k1[i], peer_k2[i], peer_u[i], peer_v[i])
            xc = _post_norm(xc, cg2 * yc, ln_g[i, 1], ln_b[i, 1])
    return xl
```

```python
import functools
import math

import jax
import jax.numpy as jnp
from jax import lax
from jax.experimental import pallas as pl
from jax.experimental.pallas import tpu as pltpu

F32 = jnp.float32
BF16 = jnp.bfloat16
I32 = jnp.int32

LN_EPS = 1e-6
DEPTH = 4
ALPHA = (2.0 * DEPTH) ** 0.25
GRID_W = 64
ROPE_DIM = 64
ROPE_BASE = 10000.0
GM_CHUNK = 128
MLA_NOPE = 128
MLA_ROPE = 64
DA_HEAD = 64
PEER_HEADS = 8
PEER_TOPK = 16
PEER_NKEYS = 128

LANES = 128
TM = 512
MIB = 1024 * 1024

NT_DIMS = (((1,), (1,)), ((), ()))


def _cparams(sem, vmem_mib):
    return pltpu.CompilerParams(dimension_semantics=sem, vmem_limit_bytes=vmem_mib * MIB)


def _ln(xf):
    mu = jnp.mean(xf, axis=-1, keepdims=True)
    var = jnp.mean(jnp.square(xf - mu), axis=-1, keepdims=True)
    return (xf - mu) * lax.rsqrt(var + LN_EPS)


def _rms(xf):
    return xf * lax.rsqrt(jnp.mean(jnp.square(xf), axis=-1, keepdims=True) + LN_EPS)


def _swap16(x):
    n = x.shape[-1]
    lane = lax.broadcasted_iota(I32, x.shape, x.ndim - 1)
    up = pltpu.roll(x, n - 16, x.ndim - 1)
    dn = pltpu.roll(x, 16, x.ndim - 1)
    return jnp.where((lane & 16) == 0, up, dn)


def _rope(x, cos, sin):
    reps = x.shape[-1] // LANES
    if reps > 1:
        cos = jnp.concatenate([cos] * reps, axis=-1)
        sin = jnp.concatenate([sin] * reps, axis=-1)
    return x * cos + _swap16(x) * sin


def _mod_kernel(c_ref, w_ref, b_ref, o_ref):
    c = c_ref[...]
    s = c * jax.nn.sigmoid(c)
    o_ref[...] = jnp.dot(s.astype(BF16), w_ref[...].astype(BF16),
                         preferred_element_type=F32) + b_ref[...]


def _mods(c_all, mod_w, mod_b):
    depth, d, n = mod_w.shape
    tn = 1024
    return pl.pallas_call(
        _mod_kernel,
        grid=(depth, n // tn),
        in_specs=[
            pl.BlockSpec((8, d), lambda l, j: (0, 0)),
            pl.BlockSpec((None, d, tn), lambda l, j: (l, 0, j)),
            pl.BlockSpec((None, 1, tn), lambda l, j: (l, 0, j)),
        ],
        out_specs=pl.BlockSpec((None, 8, tn), lambda l, j: (l, 0, j)),
        out_shape=jax.ShapeDtypeStruct((depth, 8, n), F32),
        compiler_params=_cparams(("parallel", "arbitrary"), 40),
        name="mods",
    )(c_all, mod_w, mod_b.reshape(depth, 1, n))


def _grp(i, tm, lat_rows):
    return jnp.minimum((i * tm) // lat_rows, 2)


def _mod_spec(layer, k, d, tm, lat_rows):
    return pl.BlockSpec((None, None, 1, d), lambda i, *_: (layer, _grp(i, tm, lat_rows), 0, k))


def _affine_spec(layer, which, d):
    return pl.BlockSpec((None, None, 1, d), lambda i, *_: (layer, which, 0, 0))


def _lnmod_mm_kernel(*refs, epilogue, n_rope_tiles, q_tiles, q_scale):
    if epilogue == "rope":
        x_ref, sh_ref, sc_ref, w_ref, cos_ref, sin_ref, o_ref, h_ref = refs
    else:
        x_ref, sh_ref, sc_ref, w_ref, o_ref, h_ref = refs
    j = pl.program_id(1)

    @pl.when(j == 0)
    def _():
        h = _ln(x_ref[...]) * (1.0 + sc_ref[...]) + sh_ref[...]
        h_ref[...] = h.astype(BF16)

    acc = jnp.dot(h_ref[...], w_ref[...], preferred_element_type=F32)
    if epilogue == "gelu":
        o_ref[...] = jax.nn.gelu(acc).astype(o_ref.dtype)
    elif epilogue == "rope":
        @pl.when(j < q_tiles)
        def _():
            o_ref[...] = (_rope(acc, cos_ref[...], sin_ref[...]) * q_scale).astype(o_ref.dtype)

        @pl.when(jnp.logical_and(j >= q_tiles, j < n_rope_tiles))
        def _():
            o_ref[...] = _rope(acc, cos_ref[...], sin_ref[...]).astype(o_ref.dtype)

        @pl.when(j >= n_rope_tiles)
        def _():
            o_ref[...] = acc.astype(o_ref.dtype)
    else:
        o_ref[...] = acc.astype(o_ref.dtype)


def _lnmod_mm(x, mods4, layer, k_shift, w, nt, lat_rows, out_dtype, tn, epilogue="none",
              rope_tabs=None, n_rope_tiles=0, q_tiles=0, q_scale=1.0):
    d = x.shape[1]
    n = w.shape[1]
    tm = TM
    in_specs = [
        pl.BlockSpec((tm, d), lambda i, j: (i, 0)),
        _mod_spec(layer, k_shift, d, tm, lat_rows),
        _mod_spec(layer, k_shift + 1, d, tm, lat_rows),
        pl.BlockSpec((d, tn), lambda i, j: (0, j)),
    ]
    args = [x, mods4, mods4, w]
    if epilogue == "rope":
        in_specs += [pl.BlockSpec((tm, LANES), lambda i, j: (i, 0))] * 2
        args += list(rope_tabs)
    return pl.pallas_call(
        functools.partial(_lnmod_mm_kernel, epilogue=epilogue, n_rope_tiles=n_rope_tiles,
                          q_tiles=q_tiles, q_scale=q_scale),
        grid=(nt, n // tn),
        in_specs=in_specs,
        out_specs=pl.BlockSpec((tm, tn), lambda i, j: (i, j)),
        out_shape=jax.ShapeDtypeStruct((nt * tm, n), out_dtype),
        scratch_shapes=[pltpu.VMEM((tm, d), BF16)],
        compiler_params=_cparams(("parallel", "arbitrary"), 40),
        name="lnmod_mm_" + epilogue,
    )(*args)


def _mm_postnorm_kernel(a_ref, w_ref, x_ref, gate_ref, g_ref, b_ref, o_ref):
    y = jnp.dot(a_ref[...], w_ref[...], preferred_element_type=F32)
    z = ALPHA * x_ref[...] + gate_ref[...] * y
    o_ref[...] = _ln(z) * g_ref[...] + b_ref[...]


def _mm_postnorm(a, w, x, mods4, affine4, layer, k_gate, nt, lat_rows):
    d = x.shape[1]
    kdim = a.shape[1]
    tm = TM
    return pl.pallas_call(
        _mm_postnorm_kernel,
        grid=(nt,),
        in_specs=[
            pl.BlockSpec((tm, kdim), lambda i: (i, 0)),
            pl.BlockSpec((kdim, d), lambda i: (0, 0)),
            pl.BlockSpec((tm, d), lambda i: (i, 0)),
            _mod_spec(layer, k_gate, d, tm, lat_rows),
            _affine_spec(layer, 0, d),
            _affine_spec(layer, 0, d),
        ],
        out_specs=pl.BlockSpec((tm, d), lambda i: (i, 0)),
        out_shape=jax.ShapeDtypeStruct((nt * tm, d), F32),
        compiler_params=_cparams(("parallel",), 48),
        name="mm_postnorm",
    )(a, w, x, mods4, affine4[0], affine4[1])


def _gmlp_out_kernel(u_ref, v_ref, lg_ref, lb_ref, ws_ref, bs_ref, w_ref, x_ref, gate_ref,
                     g_ref, b_ref, o_ref, vn_ref, t_ref):
    tm, gd = u_ref.shape
    groups = gd // LANES
    vn_ref[...] = (_ln(v_ref[...].astype(F32)) * lg_ref[...] + lb_ref[...]).astype(BF16)
    for c in range(tm // GM_CHUNK):
        rows = pl.ds(c * GM_CHUNK, GM_CHUNK)
        for g in range(groups):
            cols = pl.ds(g * LANES, LANES)
            sv = jnp.dot(ws_ref[g], vn_ref[rows, cols], preferred_element_type=F32) + bs_ref[g]
            t_ref[rows, cols] = (u_ref[rows, cols].astype(F32) * sv).astype(BF16)
    y = jnp.dot(t_ref[...], w_ref[...], preferred_element_type=F32)
    z = ALPHA * x_ref[...] + gate_ref[...] * y
    o_ref[...] = _ln(z) * g_ref[...] + b_ref[...]


def _gmlp_out(z, ln_g, ln_b, ws, bsb, w_out, x, mods4, affine4, layer, nt, lat_rows):
    d = x.shape[1]
    gd = z.shape[1] // 2
    groups = gd // LANES
    tm = TM // 2
    nt = nt * 2
    return pl.pallas_call(
        _gmlp_out_kernel,
        grid=(nt,),
        in_specs=[
            pl.BlockSpec((tm, gd), lambda i: (i, 0)),
            pl.BlockSpec((tm, gd), lambda i: (i, 1)),
            pl.BlockSpec((1, gd), lambda i: (0, 0)),
            pl.BlockSpec((1, gd), lambda i: (0, 0)),
            pl.BlockSpec((groups, GM_CHUNK, GM_CHUNK), lambda i: (0, 0, 0)),
            pl.BlockSpec((groups, GM_CHUNK, LANES), lambda i: (0, 0, 0)),
            pl.BlockSpec((gd, d), lambda i: (0, 0)),
            pl.BlockSpec((tm, d), lambda i: (i, 0)),
            _mod_spec(layer, 2, d, tm, lat_rows),
            _affine_spec(layer, 0, d),
            _affine_spec(layer, 0, d),
        ],
        out_specs=pl.BlockSpec((tm, d), lambda i: (i, 0)),
        out_shape=jax.ShapeDtypeStruct((nt * tm, d), F32),
        scratch_shapes=[pltpu.VMEM((tm, gd), BF16), pltpu.VMEM((tm, gd), BF16)],
        compiler_params=_cparams(("parallel",), 56),
        name="gmlp_out",
    )(z, z, ln_g, ln_b, ws, bsb, w_out, x, mods4, affine4[0], affine4[1])


def _rms_mm_kernel(*refs, mode, q_scale):
    if mode == "mla_q":
        c_ref, nw_ref, w_ref, cos_ref, sin_ref, o_ref, cn_ref = refs
    elif mode == "mla_k":
        c_ref, nw_ref, w_ref, kr_ref, cos_ref, sin_ref, o_ref, cn_ref = refs
    else:
        c_ref, nw_ref, w_ref, o_ref, cn_ref = refs
    j = pl.program_id(1)

    @pl.when(j == 0)
    def _():
        cn_ref[...] = (_rms(c_ref[...]) * nw_ref[...]).astype(BF16)

    acc = jnp.dot(cn_ref[...], w_ref[...], preferred_element_type=F32)
    if mode == "mla_q":
        roped = _rope(acc[:, LANES:], cos_ref[...], sin_ref[...])
        o_ref[...] = (jnp.concatenate([acc[:, :LANES], roped], axis=-1) * q_scale).astype(o_ref.dtype)
    elif mode == "mla_k":
        roped = _rope(kr_ref[...], cos_ref[...], sin_ref[...])
        o_ref[...] = jnp.concatenate([acc, roped], axis=-1).astype(o_ref.dtype)
    else:
        o_ref[...] = acc.astype(o_ref.dtype)


def _rms_mm(hd, col_block, norm_w, w, nt, mode, tn_w, tn_o, rope_tabs=None, kr_block=None, q_scale=1.0):
    rank = norm_w.shape[1]
    n_w = w.shape[1]
    tm = TM
    steps = n_w // tn_w
    in_specs = [
        pl.BlockSpec((tm, rank), lambda i, j: (i, col_block)),
        pl.BlockSpec((1, rank), lambda i, j: (0, 0)),
        pl.BlockSpec((rank, tn_w), lambda i, j: (0, j)),
    ]
    args = [hd, norm_w, w]
    if mode == "mla_k":
        in_specs.append(pl.BlockSpec((tm, LANES), lambda i, j: (i, kr_block)))
        args.append(hd)
    if mode in ("mla_q", "mla_k"):
        in_specs += [pl.BlockSpec((tm, LANES), lambda i, j: (i, 0))] * 2
        args += list(rope_tabs)
    return pl.pallas_call(
        functools.partial(_rms_mm_kernel, mode=mode, q_scale=q_scale),
        grid=(nt, steps),
        in_specs=in_specs,
        out_specs=pl.BlockSpec((tm, tn_o), lambda i, j: (i, j)),
        out_shape=jax.ShapeDtypeStruct((nt * tm, steps * tn_o), BF16),
        scratch_shapes=[pltpu.VMEM((tm, rank), BF16)],
        compiler_params=_cparams(("parallel", "arbitrary"), 32),
        name="rms_mm_" + mode,
    )(*args)


def _attn_kernel(*refs, n_streams, n_lat_q, ck, lam_init):
    if n_streams == 2:
        q_ref, kl_ref, kc_ref, vl_ref, vc_ref, lam_ref, sub_ref, o_ref = refs
    else:
        q_ref, kl_ref, kc_ref, vl_ref, vc_ref, o_ref = refs
    qi = pl.program_id(2)
    tq, dk = q_ref.shape
    dv = vl_ref.shape[1]
    q = q_ref[...]
    if n_streams == 2:
        lane = lax.broadcasted_iota(I32, (tq, dk), 1)
        zero = jnp.zeros_like(q)
        q = jnp.concatenate([jnp.where(lane < dk // 2, q, zero), jnp.where(lane >= dk // 2, q, zero)], axis=0)

    def chunk(carry, k, v):
        s = lax.dot_general(q, k, NT_DIMS, preferred_element_type=F32)
        out = []
        for t in range(n_streams):
            m, l, acc = carry[3 * t:3 * t + 3]
            st = s[t * tq:(t + 1) * tq]
            m_new = jnp.maximum(m, jnp.max(st, axis=-1, keepdims=True))
            alpha = jnp.exp(m - m_new)
            e = jnp.exp(st - m_new)
            l = alpha * l + jnp.sum(e, axis=-1, keepdims=True)
            acc = alpha * acc + jnp.dot(e.astype(BF16), v, preferred_element_type=F32)
            out += [m_new, l, acc]
        return tuple(out)

    init = []
    for _ in range(n_streams):
        init += [jnp.full((tq, 1), -jnp.inf, F32), jnp.zeros((tq, 1), F32), jnp.zeros((tq, dv), F32)]
    carry = chunk(tuple(init), kc_ref[...], vc_ref[...])

    def body(c, carry):
        rows = pl.ds(pl.multiple_of(c * ck, ck), ck)
        return chunk(carry, kl_ref[rows, :], vl_ref[rows, :])

    n_chunks = jnp.where(qi < n_lat_q, kl_ref.shape[0] // ck, 0)
    carry = lax.fori_loop(0, n_chunks, body, carry)

    if n_streams == 1:
        _, l, acc = carry
        o_ref[...] = (acc / l).astype(o_ref.dtype)
    else:
        _, l1, a1, _, l2, a2 = carry
        lp = lam_ref[...].astype(F32)
        lam = (jnp.exp(jnp.sum(lp[0:1] * lp[1:2], axis=-1, keepdims=True))
               - jnp.exp(jnp.sum(lp[2:3] * lp[3:4], axis=-1, keepdims=True)) + lam_init)
        o = a1 / l1 - lam * (a2 / l2)
        o_ref[...] = (_rms(o) * sub_ref[...] * (1.0 - lam_init)).astype(o_ref.dtype)


def _attention(q_arr, q_cb, k_arr, k_cb, v_arr, v_cb, dk, dv, heads, batch, lat_len, ctx_len,
               n_streams, extra=(), lam_init=0.0):
    tq = ctx_len
    n_lat_q = lat_len // tq
    lat_blocks = batch * n_lat_q

    def q_map(b, h, qi):
        return (jnp.where(qi < n_lat_q, b * n_lat_q + qi, lat_blocks + b), q_cb + h)

    in_specs = [
        pl.BlockSpec((tq, dk), q_map),
        pl.BlockSpec((lat_len, dk), lambda b, h, qi: (b, k_cb + h)),
        pl.BlockSpec((ctx_len, dk), lambda b, h, qi: (lat_blocks + b, k_cb + h)),
        pl.BlockSpec((lat_len, dv), lambda b, h, qi: (b, v_cb + h)),
        pl.BlockSpec((ctx_len, dv), lambda b, h, qi: (lat_blocks + b, v_cb + h)),
    ]
    args = [q_arr, k_arr, k_arr, v_arr, v_arr]
    for e in extra:
        in_specs.append(pl.BlockSpec(e.shape, lambda b, h, qi: (0, 0)))
        args.append(e)
    rows = batch * (lat_len + ctx_len)
    return pl.pallas_call(
        functools.partial(_attn_kernel, n_streams=n_streams, n_lat_q=n_lat_q, ck=1024, lam_init=lam_init),
        grid=(batch, heads, n_lat_q + 1),
        in_specs=in_specs,
        out_specs=pl.BlockSpec((tq, dv), lambda b, h, qi: (q_map(b, h, qi)[0], h)),
        out_shape=jax.ShapeDtypeStruct((rows, heads * dv), BF16),
        compiler_params=_cparams(("parallel", "parallel", "arbitrary"), 48),
        name="attention_%d" % n_streams,
    )(*args)


_PAIRS = [(r1, r2) for r1 in range(PEER_TOPK) for r2 in range(PEER_TOPK)
          if (r1 + 1) * (r2 + 1) <= PEER_TOPK]


def _peer_topk_kernel(q_ref, k1_ref, k2_ref, a_ref, b_ref, g_ref, s_ref, v_ref, i_ref):
    tt = q_ref.shape[0]
    half = q_ref.shape[1] // 2
    nk = PEER_NKEYS
    iota_n = lax.broadcasted_iota(I32, (nk, PEER_HEADS, tt), 0)

    for side, k_ref in enumerate((k1_ref, k2_ref)):
        qs = q_ref[:, side * half:(side + 1) * half]
        s = lax.dot_general(k_ref[...], qs, NT_DIMS, preferred_element_type=F32)
        s_ref[...] = s.reshape(nk, PEER_HEADS, tt)

        def extract(r, _):
            sc = s_ref[...]
            m = jnp.max(sc, axis=0)
            idx = jnp.min(jnp.where(sc == m[None], iota_n, nk), axis=0)
            s_ref[...] = jnp.where(iota_n == idx[None], -jnp.inf, sc)
            v_ref[side, r] = m
            i_ref[side, r] = idx
            return 0

        lax.fori_loop(0, PEER_TOPK, extract, 0)

    v1 = [v_ref[0, r] for r in range(PEER_TOPK)]
    v2 = [v_ref[1, r] for r in range(PEER_TOPK)]
    i1 = [i_ref[0, r] for r in range(PEER_TOPK)]
    i2 = [i_ref[1, r] for r in range(PEER_TOPK)]
    cand = [v1[r1] + v2[r2] for r1, r2 in _PAIRS]
    flat = [r1 * PEER_TOPK + r2 for r1, r2 in _PAIRS]
    best, a_sel, b_sel = [], [], []
    for _ in range(PEER_TOPK):
        m = functools.reduce(jnp.maximum, cand)
        fsel = functools.reduce(jnp.minimum,
                                [jnp.where(cv == m, f, PEER_TOPK * PEER_TOPK) for cv, f in zip(cand, flat)])
        a = jnp.zeros_like(i1[0])
        b = jnp.zeros_like(i1[0])
        for c, ((r1, r2), f) in enumerate(zip(_PAIRS, flat)):
            hit = fsel == f
            a = jnp.where(hit, i1[r1], a)
            b = jnp.where(hit, i2[r2], b)
            cand[c] = jnp.where(hit, -jnp.inf, cand[c])
        best.append(m)
        a_sel.append(a)
        b_sel.append(b)
    ex = [jnp.exp(v - best[0]) for v in best]
    denom = functools.reduce(jnp.add, ex)
    gates = [e / denom for e in ex]
    nj = PEER_TOPK * PEER_HEADS
    a_ref[...] = jnp.stack(a_sel, axis=0).reshape(nj, tt).T
    b_ref[...] = jnp.stack(b_sel, axis=0).reshape(nj, tt).T
    g_ref[...] = jnp.stack(gates, axis=0).reshape(nj, tt).T


def _peer_topk(q, k1big, k2big, nt_rows):
    tt = LANES
    nj = PEER_TOPK * PEER_HEADS
    spec = pl.BlockSpec((tt, nj), lambda i: (i, 0))
    return pl.pallas_call(
        _peer_topk_kernel,
        grid=(nt_rows // tt,),
        in_specs=[
            pl.BlockSpec((tt, q.shape[1]), lambda i: (i, 0)),
            pl.BlockSpec(k1big.shape, lambda i: (0, 0)),
            pl.BlockSpec(k2big.shape, lambda i: (0, 0)),
        ],
        out_specs=[spec, spec, spec],
        out_shape=[jax.ShapeDtypeStruct((nt_rows, nj), I32), jax.ShapeDtypeStruct((nt_rows, nj), I32),
                   jax.ShapeDtypeStruct((nt_rows, nj), F32)],
        scratch_shapes=[pltpu.VMEM((PEER_NKEYS, PEER_HEADS, tt), F32),
                        pltpu.VMEM((2, PEER_TOPK, PEER_HEADS, tt), F32),
                        pltpu.VMEM((2, PEER_TOPK, PEER_HEADS, tt), I32)],
        compiler_params=_cparams(("parallel",), 32),
        name="peer_topk",
    )(q, k1big, k2big)


def _peer_gates_kernel(a_ref, b_ref, g_ref, o_ref, s_ref):
    tt, nj = a_ref.shape
    nk = PEER_NKEYS
    key = lax.broadcasted_iota(I32, (nk, nj), 0)

    def token(t, _):
        row = pl.ds(t, 1)
        left = jnp.where(key == a_ref[row, :], g_ref[row, :], 0.0).astype(BF16)
        right = jnp.where(key == b_ref[row, :], 1.0, 0.0).astype(BF16)
        s_ref[pl.ds(pl.multiple_of(t * nk, nk), nk), :] = lax.dot_general(
            left, right, NT_DIMS, preferred_element_type=F32)
        return 0

    lax.fori_loop(0, tt, token, 0)
    for i1 in range(nk):
        o_ref[:, i1 * nk:(i1 + 1) * nk] = s_ref[pl.ds(i1, tt, stride=nk), :].astype(o_ref.dtype)


def _peer_gates(a, b, g):
    rows, nj = a.shape
    tt = 64
    nk = PEER_NKEYS
    spec = pl.BlockSpec((tt, nj), lambda i: (i, 0))
    return pl.pallas_call(
        _peer_gates_kernel,
        grid=(rows // tt,),
        in_specs=[spec, spec, spec],
        out_specs=pl.BlockSpec((tt, nk * nk), lambda i: (i, 0)),
        out_shape=jax.ShapeDtypeStruct((rows, nk * nk), BF16),
        scratch_shapes=[pltpu.VMEM((tt * nk, nk), F32)],
        compiler_params=_cparams(("parallel",), 32),
        name="peer_gates",
    )(a, b, g)


def _peer_main_kernel(x_ref, sh_ref, sc_ref, gate_ref, lg_ref, lb_ref, u_ref, v_ref, gm_ref, o_ref,
                      h_ref, acc_ref):
    e = pl.program_id(1)

    @pl.when(e == 0)
    def _():
        h = _ln(x_ref[...]) * (1.0 + sc_ref[...]) + sh_ref[...]
        h_ref[...] = h.astype(BF16)
        acc_ref[...] = jnp.zeros_like(acc_ref)

    act = jax.nn.gelu(lax.dot_general(h_ref[...], u_ref[...], NT_DIMS, preferred_element_type=F32))
    w = (act * gm_ref[...].astype(F32)).astype(BF16)
    acc_ref[...] += jnp.dot(w, v_ref[...], preferred_element_type=F32)

    @pl.when(e == pl.num_programs(1) - 1)
    def _():
        z = ALPHA * x_ref[...] + gate_ref[...] * acc_ref[...]
        o_ref[...] = _ln(z) * lg_ref[...] + lb_ref[...]


def _peer_main(x, mods4, affine4, layer, u_tab, v_tab, gmat, nt, lat_rows):
    d = x.shape[1]
    n_exp = u_tab.shape[0]
    tm = TM
    te = 512
    return pl.pallas_call(
        _peer_main_kernel,
        grid=(nt, n_exp // te),
        in_specs=[
            pl.BlockSpec((tm, d), lambda i, e: (i, 0)),
            _mod_spec(layer, 3, d, tm, lat_rows),
            _mod_spec(layer, 4, d, tm, lat_rows),
            _mod_spec(layer, 5, d, tm, lat_rows),
            _affine_spec(layer, 1, d),
            _affine_spec(layer, 1, d),
            pl.BlockSpec((te, d), lambda i, e: (e, 0)),
            pl.BlockSpec((te, d), lambda i, e: (e, 0)),
            pl.BlockSpec((tm, te), lambda i, e: (i, e)),
        ],
        out_specs=pl.BlockSpec((tm, d), lambda i, e: (i, 0)),
        out_shape=jax.ShapeDtypeStruct((nt * tm, d), F32),
        scratch_shapes=[pltpu.VMEM((tm, d), BF16), pltpu.VMEM((tm, d), F32)],
        compiler_params=_cparams(("parallel", "arbitrary"), 48),
        name="peer_main",
    )(x, mods4, mods4, mods4, affine4[0], affine4[1], u_tab, v_tab, gmat)


def _peer_ffn(x, mods4, affine4, layer, wq, k1big, k2big, u_tab, v_tab, nt, lat_rows):
    q = _lnmod_mm(x, mods4, layer, 3, wq, nt, lat_rows, BF16, tn=512)
    a, b, g = _peer_topk(q, k1big, k2big, nt * TM)
    gmat = _peer_gates(a, b, g)
    return _peer_main(x, mods4, affine4, layer, u_tab, v_tab, gmat, nt, lat_rows)


def _rope_tables(lat_len, batch, ctx_rows):
    rows = lat_len // GRID_W
    row = jnp.repeat(jnp.arange(rows, dtype=F32), GRID_W)
    col = jnp.tile(jnp.arange(GRID_W, dtype=F32), rows)
    nf = ROPE_DIM // 4
    inv = ROPE_BASE ** (-jnp.arange(nf, dtype=F32) / nf)
    ang_r = row[:, None] * inv
    ang_c = col[:, None] * inv
    cr, sr, cc, sc = jnp.cos(ang_r), jnp.sin(ang_r), jnp.cos(ang_c), jnp.sin(ang_c)
    cos64 = jnp.concatenate([cr, cr, cc, cc], axis=-1)
    sin64 = jnp.concatenate([-sr, sr, -sc, sc], axis=-1)
    cos = jnp.tile(jnp.concatenate([cos64, cos64], axis=-1), (batch, 1))
    sin = jnp.tile(jnp.concatenate([sin64, sin64], axis=-1), (batch, 1))
    cos = jnp.concatenate([cos, jnp.ones((ctx_rows, LANES), F32)], axis=0)
    sin = jnp.concatenate([sin, jnp.zeros((ctx_rows, LANES), F32)], axis=0)
    return cos, sin


def _expand_keys(k):
    nk, dk = k.shape
    eye = jnp.eye(PEER_HEADS, dtype=k.dtype)
    return jnp.einsum("nd,hg->nhgd", k, eye).reshape(nk * PEER_HEADS, PEER_HEADS * dk).astype(BF16)


def kernel(x, c, ctx, c_ctx, mod_w, mod_b, ln_g, ln_b, peer_wq, peer_k1, peer_k2, peer_u, peer_v,
           gm_w_in, gm_ln_g, gm_ln_b, gm_ws, gm_bs, gm_w_out,
           mla_w_in, mla_q_norm, mla_kv_norm, mla_w_uq, mla_w_ukv, mla_w_out,
           da_w_in, da_lambda, da_subln, da_w_out):
    batch, seq, d = x.shape
    ctx_len = ctx.shape[1]
    lat_rows = seq
    n_lat = batch * seq
    n_all = n_lat + batch * ctx_len
    nt_lat = n_lat // TM
    nt_all = n_all // TM

    xs = jnp.concatenate([x.reshape(n_lat, d), ctx.reshape(batch * ctx_len, d)], axis=0)
    c_all = jnp.concatenate([c, c_ctx[None], jnp.zeros((8 - batch - 1, d), F32)], axis=0)
    mods4 = _mods(c_all, mod_w, mod_b).reshape(DEPTH, 8, 1, 6 * d)
    rope_tabs = _rope_tables(seq, batch, batch * ctx_len)

    for i in range(DEPTH):
        kind = i % 3
        j = i // 3
        last = i == DEPTH - 1
        nt = nt_lat if last else nt_all
        affine4 = (ln_g.reshape(DEPTH, 2, 1, d), ln_b.reshape(DEPTH, 2, 1, d))
        if kind == 0:
            z = _lnmod_mm(xs, mods4, i, 0, gm_w_in[j].astype(BF16), nt, lat_rows, F32, tn=512,
                          epilogue="gelu")
            groups = gm_ws.shape[1]
            bsb = jnp.broadcast_to(gm_bs[j][:, :, None], (groups, GM_CHUNK, LANES))
            xs_new = _gmlp_out(z, gm_ln_g[j][None], gm_ln_b[j][None], gm_ws[j].astype(BF16), bsb,
                               gm_w_out[j].astype(BF16), xs, mods4, affine4, i, nt, lat_rows)
        elif kind == 1:
            heads = mla_w_out.shape[1] // LANES
            qr = mla_q_norm.shape[1]
            kvr = mla_kv_norm.shape[1]
            w_in = jnp.pad(mla_w_in[j], ((0, 0), (0, LANES - MLA_ROPE))).astype(BF16)
            hd = _lnmod_mm(xs, mods4, i, 0, w_in, nt, lat_rows, F32, tn=w_in.shape[1] // 3)
            wq = mla_w_uq[j].reshape(qr, heads, MLA_NOPE + MLA_ROPE)
            wq = jnp.pad(wq, ((0, 0), (0, 0), (0, 2 * LANES - MLA_NOPE - MLA_ROPE)))
            wq = wq.reshape(qr, heads * 2 * LANES).astype(BF16)
            wkv = mla_w_ukv[j].reshape(kvr, heads, 2 * LANES)
            wk = wkv[:, :, :MLA_NOPE].reshape(kvr, heads * LANES).astype(BF16)
            wv = wkv[:, :, MLA_NOPE:].reshape(kvr, heads * LANES).astype(BF16)
            scale = (MLA_NOPE + MLA_ROPE) ** -0.5
            qm = _rms_mm(hd, 0, mla_q_norm[j][None], wq, nt, "mla_q", 2 * LANES, 2 * LANES,
                         rope_tabs=rope_tabs, q_scale=scale)
            km = _rms_mm(hd, 1, mla_kv_norm[j][None], wk, nt, "mla_k", LANES, 2 * LANES,
                         rope_tabs=rope_tabs, kr_block=(qr + kvr) // LANES)
            vm = _rms_mm(hd, 1, mla_kv_norm[j][None], wv, nt, "plain", 512, 512)
            o = _attention(qm, 0, km, 0, vm, 0, 2 * LANES, LANES, heads, batch, seq, ctx_len, 1)
            xs_new = _mm_postnorm(o, mla_w_out[j].astype(BF16), xs, mods4, affine4, i, 2, nt, lat_rows)
        else:
            heads = d // (2 * DA_HEAD)
            lam_init = 0.8 - 0.6 * math.exp(-0.3 * i)
            qkv = _lnmod_mm(xs, mods4, i, 0, da_w_in[j].astype(BF16), nt, lat_rows, BF16, tn=512,
                            epilogue="rope", rope_tabs=rope_tabs, n_rope_tiles=2 * d // 512,
                            q_tiles=d // 512, q_scale=DA_HEAD ** -0.5)
            o = _attention(qkv, 0, qkv, heads, qkv, 2 * heads, LANES, LANES, heads, batch, seq, ctx_len, 2,
                           extra=(da_lambda[j], da_subln[j][None]), lam_init=lam_init)
            xs_new = _mm_postnorm(o, da_w_out[j].astype(BF16), xs, mods4, affine4, i, 2, nt, lat_rows)
        xs = xs_new
        wq_p = peer_wq[i].reshape(d, PEER_HEADS, 2, -1).transpose(0, 2, 1, 3).reshape(d, -1).astype(BF16)
        xs = _peer_ffn(xs, mods4, affine4, i, wq_p, _expand_keys(peer_k1[i]), _expand_keys(peer_k2[i]),
                       peer_u[i].astype(BF16), peer_v[i].astype(BF16), nt, lat_rows)
    return xs[:n_lat].reshape(batch, seq, d)
```

```python
import functools
import math

import jax
import jax.numpy as jnp
from jax import lax
from jax.experimental import pallas as pl
from jax.experimental.pallas import tpu as pltpu

F32 = jnp.float32
BF16 = jnp.bfloat16
I32 = jnp.int32

LN_EPS = 1e-6
DEPTH = 4
ALPHA = (2.0 * DEPTH) ** 0.25
GRID_W = 64
ROPE_DIM = 64
ROPE_BASE = 10000.0
GM_CHUNK = 128
MLA_NOPE = 128
MLA_ROPE = 64
DA_HEAD = 64
PEER_HEADS = 8
PEER_TOPK = 16
PEER_NKEYS = 128

LANES = 128
TM = 512
MIB = 1024 * 1024

NT_DIMS = (((1,), (1,)), ((), ()))
LOG2E = math.log2(math.e)


def _cparams(sem, vmem_mib):
    return pltpu.CompilerParams(dimension_semantics=sem, vmem_limit_bytes=vmem_mib * MIB)


def _ln(xf):
    mu = jnp.mean(xf, axis=-1, keepdims=True)
    var = jnp.mean(jnp.square(xf - mu), axis=-1, keepdims=True)
    return (xf - mu) * lax.rsqrt(var + LN_EPS)


def _rms(xf):
    return xf * lax.rsqrt(jnp.mean(jnp.square(xf), axis=-1, keepdims=True) + LN_EPS)


def _swap16(x):
    n = x.shape[-1]
    lane = lax.broadcasted_iota(I32, x.shape, x.ndim - 1)
    up = pltpu.roll(x, n - 16, x.ndim - 1)
    dn = pltpu.roll(x, 16, x.ndim - 1)
    return jnp.where((lane & 16) == 0, up, dn)


def _rope(x, cos, sin):
    reps = x.shape[-1] // LANES
    if reps > 1:
        cos = jnp.concatenate([cos] * reps, axis=-1)
        sin = jnp.concatenate([sin] * reps, axis=-1)
    return x * cos + _swap16(x) * sin


def _mod_kernel(c_ref, w_ref, b_ref, o_ref):
    c = c_ref[...]
    s = c * jax.nn.sigmoid(c)
    o_ref[...] = jnp.dot(s.astype(BF16), w_ref[...].astype(BF16),
                         preferred_element_type=F32) + b_ref[...]


def _mods(c_all, mod_w, mod_b):
    depth, d, n = mod_w.shape
    tn = 1024
    return pl.pallas_call(
        _mod_kernel,
        grid=(depth, n // tn),
        in_specs=[
            pl.BlockSpec((8, d), lambda l, j: (0, 0)),
            pl.BlockSpec((None, d, tn), lambda l, j: (l, 0, j)),
            pl.BlockSpec((None, 1, tn), lambda l, j: (l, 0, j)),
        ],
        out_specs=pl.BlockSpec((None, 8, tn), lambda l, j: (l, 0, j)),
        out_shape=jax.ShapeDtypeStruct((depth, 8, n), F32),
        compiler_params=_cparams(("parallel", "arbitrary"), 40),
        name="mods",
    )(c_all, mod_w, mod_b.reshape(depth, 1, n))


def _grp(i, tm, lat_rows):
    return jnp.minimum((i * tm) // lat_rows, 2)


def _mod_spec(layer, k, d, tm, lat_rows):
    return pl.BlockSpec((None, None, 1, d), lambda i, *_: (layer, _grp(i, tm, lat_rows), 0, k))


def _affine_spec(layer, which, d):
    return pl.BlockSpec((None, None, 1, d), lambda i, *_: (layer, which, 0, 0))


def _lnmod_mm_kernel(*refs, epilogue, n_rope_tiles, q_tiles, q_scale):
    if epilogue == "rope":
        x_ref, sh_ref, sc_ref, w_ref, cos_ref, sin_ref, o_ref, h_ref = refs
    else:
        x_ref, sh_ref, sc_ref, w_ref, o_ref, h_ref = refs
    j = pl.program_id(1)

    @pl.when(j == 0)
    def _():
        h = _ln(x_ref[...]) * (1.0 + sc_ref[...]) + sh_ref[...]
        h_ref[...] = h.astype(BF16)

    acc = jnp.dot(h_ref[...], w_ref[...], preferred_element_type=F32)
    if epilogue == "gelu":
        o_ref[...] = jax.nn.gelu(acc).astype(o_ref.dtype)
    elif epilogue == "rope":
        @pl.when(j < q_tiles)
        def _():
            o_ref[...] = (_rope(acc, cos_ref[...], sin_ref[...]) * q_scale).astype(o_ref.dtype)

        @pl.when(jnp.logical_and(j >= q_tiles, j < n_rope_tiles))
        def _():
            o_ref[...] = _rope(acc, cos_ref[...], sin_ref[...]).astype(o_ref.dtype)

        @pl.when(j >= n_rope_tiles)
        def _():
            o_ref[...] = acc.astype(o_ref.dtype)
    else:
        o_ref[...] = acc.astype(o_ref.dtype)


def _lnmod_mm(x, mods4, layer, k_shift, w, nt, lat_rows, out_dtype, tn, epilogue="none",
              rope_tabs=None, n_rope_tiles=0, q_tiles=0, q_scale=1.0):
    d = x.shape[1]
    n = w.shape[1]
    tm = TM
    in_specs = [
        pl.BlockSpec((tm, d), lambda i, j: (i, 0)),
        _mod_spec(layer, k_shift, d, tm, lat_rows),
        _mod_spec(layer, k_shift + 1, d, tm, lat_rows),
        pl.BlockSpec((d, tn), lambda i, j: (0, j)),
    ]
    args = [x, mods4, mods4, w]
    if epilogue == "rope":
        in_specs += [pl.BlockSpec((tm, LANES), lambda i, j: (i, 0))] * 2
        args += list(rope_tabs)
    return pl.pallas_call(
        functools.partial(_lnmod_mm_kernel, epilogue=epilogue, n_rope_tiles=n_rope_tiles,
                          q_tiles=q_tiles, q_scale=q_scale),
        grid=(nt, n // tn),
        in_specs=in_specs,
        out_specs=pl.BlockSpec((tm, tn), lambda i, j: (i, j)),
        out_shape=jax.ShapeDtypeStruct((nt * tm, n), out_dtype),
        scratch_shapes=[pltpu.VMEM((tm, d), BF16)],
        compiler_params=_cparams(("parallel", "arbitrary"), 40),
        name="lnmod_mm_" + epilogue,
    )(*args)


def _mm_postnorm_kernel(a_ref, w_ref, x_ref, gate_ref, g_ref, b_ref, o_ref):
    y = jnp.dot(a_ref[...], w_ref[...], preferred_element_type=F32)
    z = ALPHA * x_ref[...] + gate_ref[...] * y
    o_ref[...] = _ln(z) * g_ref[...] + b_ref[...]


def _mm_postnorm(a, w, x, mods4, affine4, layer, k_gate, nt, lat_rows):
    d = x.shape[1]
    kdim = a.shape[1]
    tm = TM
    return pl.pallas_call(
        _mm_postnorm_kernel,
        grid=(nt,),
        in_specs=[
            pl.BlockSpec((tm, kdim), lambda i: (i, 0)),
            pl.BlockSpec((kdim, d), lambda i: (0, 0)),
            pl.BlockSpec((tm, d), lambda i: (i, 0)),
            _mod_spec(layer, k_gate, d, tm, lat_rows),
            _affine_spec(layer, 0, d),
            _affine_spec(layer, 0, d),
        ],
        out_specs=pl.BlockSpec((tm, d), lambda i: (i, 0)),
        out_shape=jax.ShapeDtypeStruct((nt * tm, d), F32),
        compiler_params=_cparams(("parallel",), 48),
        name="mm_postnorm",
    )(a, w, x, mods4, affine4[0], affine4[1])


def _gmlp_out_kernel(u_ref, v_ref, lg_ref, lb_ref, ws_ref, bs_ref, w_ref, x_ref, gate_ref,
                     g_ref, b_ref, o_ref, vn_ref, t_ref):
    tm, gd = u_ref.shape
    groups = gd // LANES
    vn_ref[...] = (_ln(v_ref[...].astype(F32)) * lg_ref[...] + lb_ref[...]).astype(BF16)
    for c in range(tm // GM_CHUNK):
        rows = pl.ds(c * GM_CHUNK, GM_CHUNK)
        for g in range(groups):
            cols = pl.ds(g * LANES, LANES)
            sv = jnp.dot(ws_ref[g], vn_ref[rows, cols], preferred_element_type=F32) + bs_ref[g]
            t_ref[rows, cols] = (u_ref[rows, cols].astype(F32) * sv).astype(BF16)
    y = jnp.dot(t_ref[...], w_ref[...], preferred_element_type=F32)
    z = ALPHA * x_ref[...] + gate_ref[...] * y
    o_ref[...] = _ln(z) * g_ref[...] + b_ref[...]


def _gmlp_out(z, ln_g, ln_b, ws, bsb, w_out, x, mods4, affine4, layer, nt, lat_rows):
    d = x.shape[1]
    gd = z.shape[1] // 2
    groups = gd // LANES
    tm = TM // 2
    nt = nt * 2
    return pl.pallas_call(
        _gmlp_out_kernel,
        grid=(nt,),
        in_specs=[
            pl.BlockSpec((tm, gd), lambda i: (i, 0)),
            pl.BlockSpec((tm, gd), lambda i: (i, 1)),
            pl.BlockSpec((1, gd), lambda i: (0, 0)),
            pl.BlockSpec((1, gd), lambda i: (0, 0)),
            pl.BlockSpec((groups, GM_CHUNK, GM_CHUNK), lambda i: (0, 0, 0)),
            pl.BlockSpec((groups, GM_CHUNK, LANES), lambda i: (0, 0, 0)),
            pl.BlockSpec((gd, d), lambda i: (0, 0)),
            pl.BlockSpec((tm, d), lambda i: (i, 0)),
            _mod_spec(layer, 2, d, tm, lat_rows),
            _affine_spec(layer, 0, d),
            _affine_spec(layer, 0, d),
        ],
        out_specs=pl.BlockSpec((tm, d), lambda i: (i, 0)),
        out_shape=jax.ShapeDtypeStruct((nt * tm, d), F32),
        scratch_shapes=[pltpu.VMEM((tm, gd), BF16), pltpu.VMEM((tm, gd), BF16)],
        compiler_params=_cparams(("parallel",), 56),
        name="gmlp_out",
    )(z, z, ln_g, ln_b, ws, bsb, w_out, x, mods4, affine4[0], affine4[1])


def _rms_mm_kernel(*refs, mode, q_scale):
    if mode == "mla_q":
        c_ref, nw_ref, w_ref, cos_ref, sin_ref, o_ref, cn_ref = refs
    elif mode == "mla_k":
        c_ref, nw_ref, w_ref, kr_ref, cos_ref, sin_ref, o_ref, cn_ref = refs
    else:
        c_ref, nw_ref, w_ref, o_ref, cn_ref = refs
    j = pl.program_id(1)

    @pl.when(j == 0)
    def _():
        cn_ref[...] = (_rms(c_ref[...]) * nw_ref[...]).astype(BF16)

    acc = jnp.dot(cn_ref[...], w_ref[...], preferred_element_type=F32)
    if mode == "mla_q":
        roped = _rope(acc[:, LANES:], cos_ref[...], sin_ref[...])
        o_ref[...] = (jnp.concatenate([acc[:, :LANES], roped], axis=-1) * q_scale).astype(o_ref.dtype)
    elif mode == "mla_k":
        roped = _rope(kr_ref[...], cos_ref[...], sin_ref[...])
        o_ref[...] = jnp.concatenate([acc, roped], axis=-1).astype(o_ref.dtype)
    else:
        o_ref[...] = acc.astype(o_ref.dtype)


def _rms_mm(hd, col_block, norm_w, w, nt, mode, tn_w, tn_o, rope_tabs=None, kr_block=None, q_scale=1.0):
    rank = norm_w.shape[1]
    n_w = w.shape[1]
    tm = TM
    steps = n_w // tn_w
    in_specs = [
        pl.BlockSpec((tm, rank), lambda i, j: (i, col_block)),
        pl.BlockSpec((1, rank), lambda i, j: (0, 0)),
        pl.BlockSpec((rank, tn_w), lambda i, j: (0, j)),
    ]
    args = [hd, norm_w, w]
    if mode == "mla_k":
        in_specs.append(pl.BlockSpec((tm, LANES), lambda i, j: (i, kr_block)))
        args.append(hd)
    if mode in ("mla_q", "mla_k"):
        in_specs += [pl.BlockSpec((tm, LANES), lambda i, j: (i, 0))] * 2
        args += list(rope_tabs)
    return pl.pallas_call(
        functools.partial(_rms_mm_kernel, mode=mode, q_scale=q_scale),
        grid=(nt, steps),
        in_specs=in_specs,
        out_specs=pl.BlockSpec((tm, tn_o), lambda i, j: (i, j)),
        out_shape=jax.ShapeDtypeStruct((nt * tm, steps * tn_o), BF16),
        scratch_shapes=[pltpu.VMEM((tm, rank), BF16)],
        compiler_params=_cparams(("parallel", "arbitrary"), 32),
        name="rms_mm_" + mode,
    )(*args)


def _attn_kernel(*refs, n_streams, n_lat_q, ck, unroll, lam_init):
    if n_streams == 2:
        q_ref, kl_ref, kc_ref, vl_ref, vc_ref, lam_ref, sub_ref, o_ref = refs
    else:
        q_ref, kl_ref, kc_ref, vl_ref, vc_ref, o_ref = refs
    qi = pl.program_id(2)
    tq, dk = q_ref.shape
    dv = vl_ref.shape[1]
    q = q_ref[...]
    if n_streams == 2:
        lane = lax.broadcasted_iota(I32, (tq, dk), 1)
        zero = jnp.zeros_like(q)
        q = jnp.concatenate([jnp.where(lane < dk // 2, q, zero), jnp.where(lane >= dk // 2, q, zero)], axis=0)

    def chunk(carry, k, v):
        s = lax.dot_general(q, k, NT_DIMS, preferred_element_type=F32)
        out = []
        for t in range(n_streams):
            m, l, acc = carry[3 * t:3 * t + 3]
            st = s[t * tq:(t + 1) * tq]
            m_new = jnp.maximum(m, jnp.max(st, axis=-1, keepdims=True))
            alpha = jnp.exp2(m - m_new)
            e = jnp.exp2(st - m_new)
            l = alpha * l + jnp.sum(e, axis=-1, keepdims=True)
            acc = alpha * acc + jnp.dot(e.astype(BF16), v, preferred_element_type=F32)
            out += [m_new, l, acc]
        return tuple(out)

    init = []
    for _ in range(n_streams):
        init += [jnp.full((tq, 1), -jnp.inf, F32), jnp.zeros((tq, 1), F32), jnp.zeros((tq, dv), F32)]
    carry = chunk(tuple(init), kc_ref[...], vc_ref[...])

    def body(c, carry):
        for u in range(unroll):
            rows = pl.ds(pl.multiple_of((c * unroll + u) * ck, ck), ck)
            carry = chunk(carry, kl_ref[rows, :], vl_ref[rows, :])
        return carry

    n_iters = jnp.where(qi < n_lat_q, kl_ref.shape[0] // (ck * unroll), 0)
    carry = lax.fori_loop(0, n_iters, body, carry)

    if n_streams == 1:
        _, l, acc = carry
        o_ref[...] = (acc / l).astype(o_ref.dtype)
    else:
        _, l1, a1, _, l2, a2 = carry
        lp = lam_ref[...].astype(F32)
        lam = (jnp.exp(jnp.sum(lp[0:1] * lp[1:2], axis=-1, keepdims=True))
               - jnp.exp(jnp.sum(lp[2:3] * lp[3:4], axis=-1, keepdims=True)) + lam_init)
        o = a1 / l1 - lam * (a2 / l2)
        o_ref[...] = (_rms(o) * sub_ref[...] * (1.0 - lam_init)).astype(o_ref.dtype)


def _attention(q_arr, q_cb, k_arr, k_cb, v_arr, v_cb, dk, dv, heads, batch, lat_len, ctx_len,
               n_streams, extra=(), lam_init=0.0):
    tq = ctx_len
    n_lat_q = lat_len // tq
    lat_blocks = batch * n_lat_q

    def q_map(b, h, qi):
        return (jnp.where(qi < n_lat_q, b * n_lat_q + qi, lat_blocks + b), q_cb + h)

    in_specs = [
        pl.BlockSpec((tq, dk), q_map),
        pl.BlockSpec((lat_len, dk), lambda b, h, qi: (b, k_cb + h)),
        pl.BlockSpec((ctx_len, dk), lambda b, h, qi: (lat_blocks + b, k_cb + h)),
        pl.BlockSpec((lat_len, dv), lambda b, h, qi: (b, v_cb + h)),
        pl.BlockSpec((ctx_len, dv), lambda b, h, qi: (lat_blocks + b, v_cb + h)),
    ]
    args = [q_arr, k_arr, k_arr, v_arr, v_arr]
    for e in extra:
        in_specs.append(pl.BlockSpec(e.shape, lambda b, h, qi: (0, 0)))
        args.append(e)
    rows = batch * (lat_len + ctx_len)
    return pl.pallas_call(
        functools.partial(_attn_kernel, n_streams=n_streams, n_lat_q=n_lat_q, ck=1024, unroll=4,
                          lam_init=lam_init),
        grid=(batch, heads, n_lat_q + 1),
        in_specs=in_specs,
        out_specs=pl.BlockSpec((tq, dv), lambda b, h, qi: (q_map(b, h, qi)[0], h)),
        out_shape=jax.ShapeDtypeStruct((rows, heads * dv), BF16),
        compiler_params=_cparams(("parallel", "parallel", "arbitrary"), 48),
        name="attention_%d" % n_streams,
    )(*args)


_PAIRS = [(r1, r2) for r1 in range(PEER_TOPK) for r2 in range(PEER_TOPK)
          if (r1 + 1) * (r2 + 1) <= PEER_TOPK]


def _peer_topk_kernel(q_ref, k1_ref, k2_ref, a_ref, b_ref, g_ref, s_ref, v_ref, i_ref):
    tt = q_ref.shape[0]
    half = q_ref.shape[1] // 2
    nk = PEER_NKEYS
    n_chain = 4

    for side, k_ref in enumerate((k1_ref, k2_ref)):
        qs = q_ref[:, side * half:(side + 1) * half]
        s = lax.dot_general(k_ref[...], qs, NT_DIMS, preferred_element_type=F32)
        s_ref[...] = s.reshape(nk, PEER_HEADS, tt)

        def extract(r, prev):
            ms = [jnp.full(prev.shape, -jnp.inf, F32) for _ in range(n_chain)]
            ids = [jnp.zeros(prev.shape, I32) for _ in range(n_chain)]
            for n in range(nk):
                sn = jnp.where(prev == n, -jnp.inf, s_ref[n])
                s_ref[n] = sn
                c = n % n_chain
                gt = sn > ms[c]
                ms[c] = jnp.where(gt, sn, ms[c])
                ids[c] = jnp.where(gt, n, ids[c])
            while len(ms) > 1:
                m_a, m_b, i_a, i_b = ms[0], ms[1], ids[0], ids[1]
                take_b = jnp.logical_or(m_b > m_a, jnp.logical_and(m_b == m_a, i_b < i_a))
                ms = ms[2:] + [jnp.where(take_b, m_b, m_a)]
                ids = ids[2:] + [jnp.where(take_b, i_b, i_a)]
            v_ref[side, r] = ms[0]
            i_ref[side, r] = ids[0]
            return ids[0]

        lax.fori_loop(0, PEER_TOPK, extract, jnp.full((PEER_HEADS, tt), -1, I32))

    v1 = [v_ref[0, r] for r in range(PEER_TOPK)]
    v2 = [v_ref[1, r] for r in range(PEER_TOPK)]
    i1 = [i_ref[0, r] for r in range(PEER_TOPK)]
    i2 = [i_ref[1, r] for r in range(PEER_TOPK)]
    cand = [v1[r1] + v2[r2] for r1, r2 in _PAIRS]
    flat = [r1 * PEER_TOPK + r2 for r1, r2 in _PAIRS]
    best, a_sel, b_sel = [], [], []
    for _ in range(PEER_TOPK):
        m = functools.reduce(jnp.maximum, cand)
        fsel = functools.reduce(jnp.minimum,
                                [jnp.where(cv == m, f, PEER_TOPK * PEER_TOPK) for cv, f in zip(cand, flat)])
        a = jnp.zeros_like(i1[0])
        b = jnp.zeros_like(i1[0])
        for c, ((r1, r2), f) in enumerate(zip(_PAIRS, flat)):
            hit = fsel == f
            a = jnp.where(hit, i1[r1], a)
            b = jnp.where(hit, i2[r2], b)
            cand[c] = jnp.where(hit, -jnp.inf, cand[c])
        best.append(m)
        a_sel.append(a)
        b_sel.append(b)
    ex = [jnp.exp(v - best[0]) for v in best]
    denom = functools.reduce(jnp.add, ex)
    gates = [e / denom for e in ex]
    nj = PEER_TOPK * PEER_HEADS
    a_ref[...] = jnp.stack(a_sel, axis=0).reshape(nj, tt).T
    b_ref[...] = jnp.stack(b_sel, axis=0).reshape(nj, tt).T
    g_ref[...] = jnp.stack(gates, axis=0).reshape(nj, tt).T


def _peer_topk(q, k1big, k2big, nt_rows):
    tt = LANES
    nj = PEER_TOPK * PEER_HEADS
    spec = pl.BlockSpec((tt, nj), lambda i: (i, 0))
    return pl.pallas_call(
        _peer_topk_kernel,
        grid=(nt_rows // tt,),
        in_specs=[
            pl.BlockSpec((tt, q.shape[1]), lambda i: (i, 0)),
            pl.BlockSpec(k1big.shape, lambda i: (0, 0)),
            pl.BlockSpec(k2big.shape, lambda i: (0, 0)),
        ],
        out_specs=[spec, spec, spec],
        out_shape=[jax.ShapeDtypeStruct((nt_rows, nj), I32), jax.ShapeDtypeStruct((nt_rows, nj), I32),
                   jax.ShapeDtypeStruct((nt_rows, nj), F32)],
        scratch_shapes=[pltpu.VMEM((PEER_NKEYS, PEER_HEADS, tt), F32),
                        pltpu.VMEM((2, PEER_TOPK, PEER_HEADS, tt), F32),
                        pltpu.VMEM((2, PEER_TOPK, PEER_HEADS, tt), I32)],
        compiler_params=_cparams(("parallel",), 32),
        name="peer_topk",
    )(q, k1big, k2big)


def _peer_gates_kernel(a_ref, b_ref, g_ref, o_ref, s_ref):
    tt, nj = a_ref.shape
    nk = PEER_NKEYS
    key = lax.broadcasted_iota(I32, (nk, nj), 0)

    def token(t, _):
        row = pl.ds(t, 1)
        left = jnp.where(key == a_ref[row, :], g_ref[row, :], 0.0).astype(BF16)
        right = jnp.where(key == b_ref[row, :], 1.0, 0.0).astype(BF16)
        s_ref[pl.ds(pl.multiple_of(t * nk, nk), nk), :] = lax.dot_general(
            left, right, NT_DIMS, preferred_element_type=F32)
        return 0

    lax.fori_loop(0, tt, token, 0, unroll=8)
    for i1 in range(nk):
        o_ref[:, i1 * nk:(i1 + 1) * nk] = s_ref[pl.ds(i1, tt, stride=nk), :].astype(o_ref.dtype)


def _peer_gates(a, b, g):
    rows, nj = a.shape
    tt = 64
    nk = PEER_NKEYS
    spec = pl.BlockSpec((tt, nj), lambda i: (i, 0))
    return pl.pallas_call(
        _peer_gates_kernel,
        grid=(rows // tt,),
        in_specs=[spec, spec, spec],
        out_specs=pl.BlockSpec((tt, nk * nk), lambda i: (i, 0)),
        out_shape=jax.ShapeDtypeStruct((rows, nk * nk), BF16),
        scratch_shapes=[pltpu.VMEM((tt * nk, nk), F32)],
        compiler_params=_cparams(("parallel",), 32),
        name="peer_gates",
    )(a, b, g)


def _peer_main_kernel(x_ref, sh_ref, sc_ref, gate_ref, lg_ref, lb_ref, u_ref, v_ref, gm_ref, o_ref,
                      h_ref, acc_ref):
    e = pl.program_id(1)

    @pl.when(e == 0)
    def _():
        h = _ln(x_ref[...]) * (1.0 + sc_ref[...]) + sh_ref[...]
        h_ref[...] = h.astype(BF16)
        acc_ref[...] = jnp.zeros_like(acc_ref)

    act = jax.nn.gelu(lax.dot_general(h_ref[...], u_ref[...], NT_DIMS, preferred_element_type=F32))
    w = (act * gm_ref[...].astype(F32)).astype(BF16)
    acc_ref[...] += jnp.dot(w, v_ref[...], preferred_element_type=F32)

    @pl.when(e == pl.num_programs(1) - 1)
    def _():
        z = ALPHA * x_ref[...] + gate_ref[...] * acc_ref[...]
        o_ref[...] = _ln(z) * lg_ref[...] + lb_ref[...]


def _peer_main(x, mods4, affine4, layer, u_tab, v_tab, gmat, nt, lat_rows):
    d = x.shape[1]
    n_exp = u_tab.shape[0]
    tm = TM
    te = 512
    return pl.pallas_call(
        _peer_main_kernel,
        grid=(nt, n_exp // te),
        in_specs=[
            pl.BlockSpec((tm, d), lambda i, e: (i, 0)),
            _mod_spec(layer, 3, d, tm, lat_rows),
            _mod_spec(layer, 4, d, tm, lat_rows),
            _mod_spec(layer, 5, d, tm, lat_rows),
            _affine_spec(layer, 1, d),
            _affine_spec(layer, 1, d),
            pl.BlockSpec((te, d), lambda i, e: (e, 0)),
            pl.BlockSpec((te, d), lambda i, e: (e, 0)),
            pl.BlockSpec((tm, te), lambda i, e: (i, e)),
        ],
        out_specs=pl.BlockSpec((tm, d), lambda i, e: (i, 0)),
        out_shape=jax.ShapeDtypeStruct((nt * tm, d), F32),
        scratch_shapes=[pltpu.VMEM((tm, d), BF16), pltpu.VMEM((tm, d), F32)],
        compiler_params=_cparams(("parallel", "arbitrary"), 48),
        name="peer_main",
    )(x, mods4, mods4, mods4, affine4[0], affine4[1], u_tab, v_tab, gmat)


def _peer_ffn(x, mods4, affine4, layer, wq, k1big, k2big, u_tab, v_tab, nt, lat_rows):
    q = _lnmod_mm(x, mods4, layer, 3, wq, nt, lat_rows, BF16, tn=512)
    a, b, g = _peer_topk(q, k1big, k2big, nt * TM)
    gmat = _peer_gates(a, b, g)
    return _peer_main(x, mods4, affine4, layer, u_tab, v_tab, gmat, nt, lat_rows)


def _rope_tables(lat_len, batch, ctx_rows):
    rows = lat_len // GRID_W
    row = jnp.repeat(jnp.arange(rows, dtype=F32), GRID_W)
    col = jnp.tile(jnp.arange(GRID_W, dtype=F32), rows)
    nf = ROPE_DIM // 4
    inv = ROPE_BASE ** (-jnp.arange(nf, dtype=F32) / nf)
    ang_r = row[:, None] * inv
    ang_c = col[:, None] * inv
    cr, sr, cc, sc = jnp.cos(ang_r), jnp.sin(ang_r), jnp.cos(ang_c), jnp.sin(ang_c)
    cos64 = jnp.concatenate([cr, cr, cc, cc], axis=-1)
    sin64 = jnp.concatenate([-sr, sr, -sc, sc], axis=-1)
    cos = jnp.tile(jnp.concatenate([cos64, cos64], axis=-1), (batch, 1))
    sin = jnp.tile(jnp.concatenate([sin64, sin64], axis=-1), (batch, 1))
    cos = jnp.concatenate([cos, jnp.ones((ctx_rows, LANES), F32)], axis=0)
    sin = jnp.concatenate([sin, jnp.zeros((ctx_rows, LANES), F32)], axis=0)
    return cos, sin


def _expand_keys(k):
    nk, dk = k.shape
    eye = jnp.eye(PEER_HEADS, dtype=k.dtype)
    return jnp.einsum("nd,hg->nhgd", k, eye).reshape(nk * PEER_HEADS, PEER_HEADS * dk).astype(BF16)


def kernel(x, c, ctx, c_ctx, mod_w, mod_b, ln_g, ln_b, peer_wq, peer_k1, peer_k2, peer_u, peer_v,
           gm_w_in, gm_ln_g, gm_ln_b, gm_ws, gm_bs, gm_w_out,
           mla_w_in, mla_q_norm, mla_kv_norm, mla_w_uq, mla_w_ukv, mla_w_out,
           da_w_in, da_lambda, da_subln, da_w_out):
    batch, seq, d = x.shape
    ctx_len = ctx.shape[1]
    lat_rows = seq
    n_lat = batch * seq
    n_all = n_lat + batch * ctx_len
    nt_lat = n_lat // TM
    nt_all = n_all // TM

    xs = jnp.concatenate([x.reshape(n_lat, d), ctx.reshape(batch * ctx_len, d)], axis=0)
    c_all = jnp.concatenate([c, c_ctx[None], jnp.zeros((8 - batch - 1, d), F32)], axis=0)
    mods4 = _mods(c_all, mod_w, mod_b).reshape(DEPTH, 8, 1, 6 * d)
    rope_tabs = _rope_tables(seq, batch, batch * ctx_len)

    for i in range(DEPTH):
        kind = i % 3
        j = i // 3
        last = i == DEPTH - 1
        nt = nt_lat if last else nt_all
        affine4 = (ln_g.reshape(DEPTH, 2, 1, d), ln_b.reshape(DEPTH, 2, 1, d))
        if kind == 0:
            z = _lnmod_mm(xs, mods4, i, 0, gm_w_in[j].astype(BF16), nt, lat_rows, F32, tn=512,
                          epilogue="gelu")
            groups = gm_ws.shape[1]
            bsb = jnp.broadcast_to(gm_bs[j][:, :, None], (groups, GM_CHUNK, LANES))
            xs_new = _gmlp_out(z, gm_ln_g[j][None], gm_ln_b[j][None], gm_ws[j].astype(BF16), bsb,
                               gm_w_out[j].astype(BF16), xs, mods4, affine4, i, nt, lat_rows)
        elif kind == 1:
            heads = mla_w_out.shape[1] // LANES
            qr = mla_q_norm.shape[1]
            kvr = mla_kv_norm.shape[1]
            w_in = jnp.pad(mla_w_in[j], ((0, 0), (0, LANES - MLA_ROPE))).astype(BF16)
            hd = _lnmod_mm(xs, mods4, i, 0, w_in, nt, lat_rows, F32, tn=w_in.shape[1] // 3)
            wq = mla_w_uq[j].reshape(qr, heads, MLA_NOPE + MLA_ROPE)
            wq = jnp.pad(wq, ((0, 0), (0, 0), (0, 2 * LANES - MLA_NOPE - MLA_ROPE)))
            wq = wq.reshape(qr, heads * 2 * LANES).astype(BF16)
            wkv = mla_w_ukv[j].reshape(kvr, heads, 2 * LANES)
            wk = wkv[:, :, :MLA_NOPE].reshape(kvr, heads * LANES).astype(BF16)
            wv = wkv[:, :, MLA_NOPE:].reshape(kvr, heads * LANES).astype(BF16)
            scale = (MLA_NOPE + MLA_ROPE) ** -0.5 * LOG2E
            qm = _rms_mm(hd, 0, mla_q_norm[j][None], wq, nt, "mla_q", 2 * LANES, 2 * LANES,
                         rope_tabs=rope_tabs, q_scale=scale)
            km = _rms_mm(hd, 1, mla_kv_norm[j][None], wk, nt, "mla_k", LANES, 2 * LANES,
                         rope_tabs=rope_tabs, kr_block=(qr + kvr) // LANES)
            vm = _rms_mm(hd, 1, mla_kv_norm[j][None], wv, nt, "plain", 512, 512)
            o = _attention(qm, 0, km, 0, vm, 0, 2 * LANES, LANES, heads, batch, seq, ctx_len, 1)
            xs_new = _mm_postnorm(o, mla_w_out[j].astype(BF16), xs, mods4, affine4, i, 2, nt, lat_rows)
        else:
            heads = d // (2 * DA_HEAD)
            lam_init = 0.8 - 0.6 * math.exp(-0.3 * i)
            qkv = _lnmod_mm(xs, mods4, i, 0, da_w_in[j].astype(BF16), nt, lat_rows, BF16, tn=512,
                            epilogue="rope", rope_tabs=rope_tabs, n_rope_tiles=2 * d // 512,
                            q_tiles=d // 512, q_scale=DA_HEAD ** -0.5 * LOG2E)
            o = _attention(qkv, 0, qkv, heads, qkv, 2 * heads, LANES, LANES, heads, batch, seq, ctx_len, 2,
                           extra=(da_lambda[j], da_subln[j][None]), lam_init=lam_init)
            xs_new = _mm_postnorm(o, da_w_out[j].astype(BF16), xs, mods4, affine4, i, 2, nt, lat_rows)
        xs = xs_new
        wq_p = peer_wq[i].reshape(d, PEER_HEADS, 2, -1).transpose(0, 2, 1, 3).reshape(d, -1).astype(BF16)
        xs = _peer_ffn(xs, mods4, affine4, i, wq_p, _expand_keys(peer_k1[i]), _expand_keys(peer_k2[i]),
                       peer_u[i].astype(BF16), peer_v[i].astype(BF16), nt, lat_rows)
    return xs[:n_lat].reshape(batch, seq, d)
```

```python
import functools
import math

import jax
import jax.numpy as jnp
from jax import lax
from jax.experimental import pallas as pl
from jax.experimental.pallas import tpu as pltpu

F32 = jnp.float32
BF16 = jnp.bfloat16
I32 = jnp.int32

LN_EPS = 1e-6
DEPTH = 4
ALPHA = (2.0 * DEPTH) ** 0.25
GRID_W = 64
ROPE_DIM = 64
ROPE_BASE = 10000.0
GM_CHUNK = 128
MLA_NOPE = 128
MLA_ROPE = 64
DA_HEAD = 64
PEER_HEADS = 8
PEER_TOPK = 16
PEER_NKEYS = 128

LANES = 128
TM = 512
MIB = 1024 * 1024

NT_DIMS = (((1,), (1,)), ((), ()))
LOG2E = math.log2(math.e)


def _cparams(sem, vmem_mib):
    return pltpu.CompilerParams(dimension_semantics=sem, vmem_limit_bytes=vmem_mib * MIB)


def _ln(xf):
    mu = jnp.mean(xf, axis=-1, keepdims=True)
    var = jnp.mean(jnp.square(xf - mu), axis=-1, keepdims=True)
    return (xf - mu) * lax.rsqrt(var + LN_EPS)


def _rms(xf):
    return xf * lax.rsqrt(jnp.mean(jnp.square(xf), axis=-1, keepdims=True) + LN_EPS)


def _swap16(x):
    n = x.shape[-1]
    lane = lax.broadcasted_iota(I32, x.shape, x.ndim - 1)
    up = pltpu.roll(x, n - 16, x.ndim - 1)
    dn = pltpu.roll(x, 16, x.ndim - 1)
    return jnp.where((lane & 16) == 0, up, dn)


def _rope(x, cos, sin):
    reps = x.shape[-1] // LANES
    if reps > 1:
        cos = jnp.concatenate([cos] * reps, axis=-1)
        sin = jnp.concatenate([sin] * reps, axis=-1)
    return x * cos + _swap16(x) * sin


def _mod_kernel(c_ref, w_ref, b_ref, o_ref):
    c = c_ref[...]
    s = c * jax.nn.sigmoid(c)
    o_ref[...] = jnp.dot(s.astype(BF16), w_ref[...].astype(BF16),
                         preferred_element_type=F32) + b_ref[...]


def _mods(c_all, mod_w, mod_b):
    depth, d, n = mod_w.shape
    tn = 1024
    return pl.pallas_call(
        _mod_kernel,
        grid=(depth, n // tn),
        in_specs=[
            pl.BlockSpec((8, d), lambda l, j: (0, 0)),
            pl.BlockSpec((None, d, tn), lambda l, j: (l, 0, j)),
            pl.BlockSpec((None, 1, tn), lambda l, j: (l, 0, j)),
        ],
        out_specs=pl.BlockSpec((None, 8, tn), lambda l, j: (l, 0, j)),
        out_shape=jax.ShapeDtypeStruct((depth, 8, n), F32),
        compiler_params=_cparams(("parallel", "arbitrary"), 40),
        name="mods",
    )(c_all, mod_w, mod_b.reshape(depth, 1, n))


def _grp(i, tm, lat_rows):
    return jnp.minimum((i * tm) // lat_rows, 2)


def _mod_spec(layer, k, d, tm, lat_rows):
    return pl.BlockSpec((None, None, 1, d), lambda i, *_: (layer, _grp(i, tm, lat_rows), 0, k))


def _affine_spec(layer, which, d):
    return pl.BlockSpec((None, None, 1, d), lambda i, *_: (layer, which, 0, 0))


def _lnmod_mm_kernel(*refs, epilogue, n_rope_tiles, q_tiles, q_scale):
    if epilogue == "rope":
        x_ref, sh_ref, sc_ref, w_ref, cos_ref, sin_ref, o_ref, h_ref = refs
    else:
        x_ref, sh_ref, sc_ref, w_ref, o_ref, h_ref = refs
    j = pl.program_id(1)

    @pl.when(j == 0)
    def _():
        h = _ln(x_ref[...]) * (1.0 + sc_ref[...]) + sh_ref[...]
        h_ref[...] = h.astype(BF16)

    acc = jnp.dot(h_ref[...], w_ref[...], preferred_element_type=F32)
    if epilogue == "gelu":
        o_ref[...] = jax.nn.gelu(acc).astype(o_ref.dtype)
    elif epilogue == "rope":
        @pl.when(j < q_tiles)
        def _():
            o_ref[...] = (_rope(acc, cos_ref[...], sin_ref[...]) * q_scale).astype(o_ref.dtype)

        @pl.when(jnp.logical_and(j >= q_tiles, j < n_rope_tiles))
        def _():
            o_ref[...] = _rope(acc, cos_ref[...], sin_ref[...]).astype(o_ref.dtype)

        @pl.when(j >= n_rope_tiles)
        def _():
            o_ref[...] = acc.astype(o_ref.dtype)
    else:
        o_ref[...] = acc.astype(o_ref.dtype)


def _lnmod_mm(x, mods4, layer, k_shift, w, nt, lat_rows, out_dtype, tn, epilogue="none",
              rope_tabs=None, n_rope_tiles=0, q_tiles=0, q_scale=1.0):
    d = x.shape[1]
    n = w.shape[1]
    tm = TM
    in_specs = [
        pl.BlockSpec((tm, d), lambda i, j: (i, 0)),
        _mod_spec(layer, k_shift, d, tm, lat_rows),
        _mod_spec(layer, k_shift + 1, d, tm, lat_rows),
        pl.BlockSpec((d, tn), lambda i, j: (0, j)),
    ]
    args = [x, mods4, mods4, w]
    if epilogue == "rope":
        in_specs += [pl.BlockSpec((tm, LANES), lambda i, j: (i, 0))] * 2
        args += list(rope_tabs)
    return pl.pallas_call(
        functools.partial(_lnmod_mm_kernel, epilogue=epilogue, n_rope_tiles=n_rope_tiles,
                          q_tiles=q_tiles, q_scale=q_scale),
        grid=(nt, n // tn),
        in_specs=in_specs,
        out_specs=pl.BlockSpec((tm, tn), lambda i, j: (i, j)),
        out_shape=jax.ShapeDtypeStruct((nt * tm, n), out_dtype),
        scratch_shapes=[pltpu.VMEM((tm, d), BF16)],
        compiler_params=_cparams(("parallel", "arbitrary"), 40),
        name="lnmod_mm_" + epilogue,
    )(*args)


def _mm_postnorm_kernel(a_ref, w_ref, x_ref, gate_ref, g_ref, b_ref, o_ref):
    y = jnp.dot(a_ref[...], w_ref[...], preferred_element_type=F32)
    z = ALPHA * x_ref[...] + gate_ref[...] * y
    o_ref[...] = _ln(z) * g_ref[...] + b_ref[...]


def _mm_postnorm(a, w, x, mods4, affine4, layer, k_gate, nt, lat_rows):
    d = x.shape[1]
    kdim = a.shape[1]
    tm = TM
    return pl.pallas_call(
        _mm_postnorm_kernel,
        grid=(nt,),
        in_specs=[
            pl.BlockSpec((tm, kdim), lambda i: (i, 0)),
            pl.BlockSpec((kdim, d), lambda i: (0, 0)),
            pl.BlockSpec((tm, d), lambda i: (i, 0)),
            _mod_spec(layer, k_gate, d, tm, lat_rows),
            _affine_spec(layer, 0, d),
            _affine_spec(layer, 0, d),
        ],
        out_specs=pl.BlockSpec((tm, d), lambda i: (i, 0)),
        out_shape=jax.ShapeDtypeStruct((nt * tm, d), F32),
        compiler_params=_cparams(("parallel",), 48),
        name="mm_postnorm",
    )(a, w, x, mods4, affine4[0], affine4[1])


def _gmlp_out_kernel(u_ref, v_ref, lg_ref, lb_ref, ws_ref, bs_ref, w_ref, x_ref, gate_ref,
                     g_ref, b_ref, o_ref, vn_ref, t_ref):
    tm, gd = u_ref.shape
    groups = gd // LANES
    vn_ref[...] = (_ln(v_ref[...].astype(F32)) * lg_ref[...] + lb_ref[...]).astype(BF16)
    for c in range(tm // GM_CHUNK):
        rows = pl.ds(c * GM_CHUNK, GM_CHUNK)
        for g in range(groups):
            cols = pl.ds(g * LANES, LANES)
            sv = jnp.dot(ws_ref[g], vn_ref[rows, cols], preferred_element_type=F32) + bs_ref[g]
            t_ref[rows, cols] = (u_ref[rows, cols].astype(F32) * sv).astype(BF16)
    y = jnp.dot(t_ref[...], w_ref[...], preferred_element_type=F32)
    z = ALPHA * x_ref[...] + gate_ref[...] * y
    o_ref[...] = _ln(z) * g_ref[...] + b_ref[...]


def _gmlp_out(z, ln_g, ln_b, ws, bsb, w_out, x, mods4, affine4, layer, nt, lat_rows):
    d = x.shape[1]
    gd = z.shape[1] // 2
    groups = gd // LANES
    tm = TM // 2
    nt = nt * 2
    return pl.pallas_call(
        _gmlp_out_kernel,
        grid=(nt,),
        in_specs=[
            pl.BlockSpec((tm, gd), lambda i: (i, 0)),
            pl.BlockSpec((tm, gd), lambda i: (i, 1)),
            pl.BlockSpec((1, gd), lambda i: (0, 0)),
            pl.BlockSpec((1, gd), lambda i: (0, 0)),
            pl.BlockSpec((groups, GM_CHUNK, GM_CHUNK), lambda i: (0, 0, 0)),
            pl.BlockSpec((groups, GM_CHUNK, LANES), lambda i: (0, 0, 0)),
            pl.BlockSpec((gd, d), lambda i: (0, 0)),
            pl.BlockSpec((tm, d), lambda i: (i, 0)),
            _mod_spec(layer, 2, d, tm, lat_rows),
            _affine_spec(layer, 0, d),
            _affine_spec(layer, 0, d),
        ],
        out_specs=pl.BlockSpec((tm, d), lambda i: (i, 0)),
        out_shape=jax.ShapeDtypeStruct((nt * tm, d), F32),
        scratch_shapes=[pltpu.VMEM((tm, gd), BF16), pltpu.VMEM((tm, gd), BF16)],
        compiler_params=_cparams(("parallel",), 56),
        name="gmlp_out",
    )(z, z, ln_g, ln_b, ws, bsb, w_out, x, mods4, affine4[0], affine4[1])


def _rms_mm_kernel(*refs, mode, q_scale):
    if mode == "mla_q":
        c_ref, nw_ref, w_ref, cos_ref, sin_ref, o_ref, cn_ref = refs
    elif mode == "mla_k":
        c_ref, nw_ref, w_ref, kr_ref, cos_ref, sin_ref, o_ref, cn_ref = refs
    else:
        c_ref, nw_ref, w_ref, o_ref, cn_ref = refs
    j = pl.program_id(1)

    @pl.when(j == 0)
    def _():
        cn_ref[...] = (_rms(c_ref[...]) * nw_ref[...]).astype(BF16)

    acc = jnp.dot(cn_ref[...], w_ref[...], preferred_element_type=F32)
    if mode == "mla_q":
        roped = _rope(acc[:, LANES:], cos_ref[...], sin_ref[...])
        o_ref[...] = (jnp.concatenate([acc[:, :LANES], roped], axis=-1) * q_scale).astype(o_ref.dtype)
    elif mode == "mla_k":
        roped = _rope(kr_ref[...], cos_ref[...], sin_ref[...])
        o_ref[...] = jnp.concatenate([acc, roped], axis=-1).astype(o_ref.dtype)
    else:
        o_ref[...] = acc.astype(o_ref.dtype)


def _rms_mm(hd, col_block, norm_w, w, nt, mode, tn_w, tn_o, rope_tabs=None, kr_block=None, q_scale=1.0):
    rank = norm_w.shape[1]
    n_w = w.shape[1]
    tm = TM
    steps = n_w // tn_w
    in_specs = [
        pl.BlockSpec((tm, rank), lambda i, j: (i, col_block)),
        pl.BlockSpec((1, rank), lambda i, j: (0, 0)),
        pl.BlockSpec((rank, tn_w), lambda i, j: (0, j)),
    ]
    args = [hd, norm_w, w]
    if mode == "mla_k":
        in_specs.append(pl.BlockSpec((tm, LANES), lambda i, j: (i, kr_block)))
        args.append(hd)
    if mode in ("mla_q", "mla_k"):
        in_specs += [pl.BlockSpec((tm, LANES), lambda i, j: (i, 0))] * 2
        args += list(rope_tabs)
    return pl.pallas_call(
        functools.partial(_rms_mm_kernel, mode=mode, q_scale=q_scale),
        grid=(nt, steps),
        in_specs=in_specs,
        out_specs=pl.BlockSpec((tm, tn_o), lambda i, j: (i, j)),
        out_shape=jax.ShapeDtypeStruct((nt * tm, steps * tn_o), BF16),
        scratch_shapes=[pltpu.VMEM((tm, rank), BF16)],
        compiler_params=_cparams(("parallel", "arbitrary"), 32),
        name="rms_mm_" + mode,
    )(*args)


def _attn_kernel(*refs, n_streams, has_lat, ck, lam_init):
    refs = list(refs)
    o_ref = refs.pop()
    q_ref = refs.pop(0)
    if has_lat:
        kl_ref, kc_ref, vl_ref, vc_ref = refs[:4]
        refs = refs[4:]
    else:
        kc_ref, vc_ref = refs[:2]
        refs = refs[2:]
    tq, dk = q_ref.shape
    dv = vc_ref.shape[1]
    q = q_ref[...]
    if n_streams == 2:
        lane = lax.broadcasted_iota(I32, (tq, dk), 1)
        zero = jnp.zeros_like(q)
        q = jnp.concatenate([jnp.where(lane < dk // 2, q, zero), jnp.where(lane >= dk // 2, q, zero)], axis=0)

    def chunk(carry, k, v):
        s = lax.dot_general(q, k, NT_DIMS, preferred_element_type=F32)
        out = []
        for t in range(n_streams):
            m, l, acc = carry[3 * t:3 * t + 3]
            st = s[t * tq:(t + 1) * tq]
            m_new = jnp.maximum(m, jnp.max(st, axis=-1, keepdims=True))
            alpha = jnp.exp2(m - m_new)
            e = jnp.exp2(st - m_new)
            l = alpha * l + jnp.sum(e, axis=-1, keepdims=True)
            acc = alpha * acc + jnp.dot(e.astype(BF16), v, preferred_element_type=F32)
            out += [m_new, l, acc]
        return tuple(out)

    init = []
    for _ in range(n_streams):
        init += [jnp.full((tq, 1), -jnp.inf, F32), jnp.zeros((tq, 1), F32), jnp.zeros((tq, dv), F32)]
    carry = chunk(tuple(init), kc_ref[...], vc_ref[...])
    if has_lat:
        for c in range(kl_ref.shape[0] // ck):
            carry = chunk(carry, kl_ref[c * ck:(c + 1) * ck, :], vl_ref[c * ck:(c + 1) * ck, :])

    if n_streams == 1:
        _, l, acc = carry
        o_ref[...] = (acc / l).astype(o_ref.dtype)
    else:
        lam_ref, sub_ref = refs
        _, l1, a1, _, l2, a2 = carry
        lp = lam_ref[...].astype(F32)
        lam = (jnp.exp(jnp.sum(lp[0:1] * lp[1:2], axis=-1, keepdims=True))
               - jnp.exp(jnp.sum(lp[2:3] * lp[3:4], axis=-1, keepdims=True)) + lam_init)
        o = a1 / l1 - lam * (a2 / l2)
        o_ref[...] = (_rms(o) * sub_ref[...] * (1.0 - lam_init)).astype(o_ref.dtype)


def _attention(q_arr, q_cb, k_arr, k_cb, v_arr, v_cb, dk, dv, heads, batch, lat_len, ctx_len,
               n_streams, extra=(), lam_init=0.0):
    ctx_block0 = batch * lat_len // ctx_len
    outs = []
    for has_lat in (True, False):
        tq = 512 if has_lat else ctx_len
        n_q = lat_len // tq if has_lat else 1
        q_block0 = 0 if has_lat else ctx_block0

        def q_map(b, h, qi, n_q=n_q, q_block0=q_block0):
            return (q_block0 + b * n_q + qi, q_cb + h)

        kc_spec = pl.BlockSpec((ctx_len, dk), lambda b, h, qi: (ctx_block0 + b, k_cb + h))
        vc_spec = pl.BlockSpec((ctx_len, dv), lambda b, h, qi: (ctx_block0 + b, v_cb + h))
        in_specs = [pl.BlockSpec((tq, dk), q_map)]
        args = [q_arr]
        if has_lat:
            in_specs += [pl.BlockSpec((lat_len, dk), lambda b, h, qi: (b, k_cb + h)), kc_spec,
                         pl.BlockSpec((lat_len, dv), lambda b, h, qi: (b, v_cb + h)), vc_spec]
            args += [k_arr, k_arr, v_arr, v_arr]
        else:
            in_specs += [kc_spec, vc_spec]
            args += [k_arr, v_arr]
        for e in extra:
            in_specs.append(pl.BlockSpec(e.shape, lambda b, h, qi: (0, 0)))
            args.append(e)
        outs.append(pl.pallas_call(
            functools.partial(_attn_kernel, n_streams=n_streams, has_lat=has_lat, ck=1024, lam_init=lam_init),
            grid=(batch, heads, n_q),
            in_specs=in_specs,
            out_specs=pl.BlockSpec((tq, dv), lambda b, h, qi, n_q=n_q: (b * n_q + qi, h)),
            out_shape=jax.ShapeDtypeStruct((batch * n_q * tq, heads * dv), BF16),
            compiler_params=_cparams(("parallel", "parallel", "arbitrary"), 56),
            name="attention_%d_%s" % (n_streams, "lat" if has_lat else "ctx"),
        )(*args))
    return jnp.concatenate(outs, axis=0)


_PAIRS = [(r1, r2) for r1 in range(PEER_TOPK) for r2 in range(PEER_TOPK)
          if (r1 + 1) * (r2 + 1) <= PEER_TOPK]


def _peer_topk_kernel(q_ref, k1_ref, k2_ref, a_ref, b_ref, g_ref, s_ref, v_ref, i_ref):
    tt = q_ref.shape[0]
    half = q_ref.shape[1] // 2
    nk = PEER_NKEYS
    n_chain = 4

    for side, k_ref in enumerate((k1_ref, k2_ref)):
        qs = q_ref[:, side * half:(side + 1) * half]
        s = lax.dot_general(k_ref[...], qs, NT_DIMS, preferred_element_type=F32)
        s_ref[...] = s.reshape(nk, PEER_HEADS, tt)

        def extract(r, prev):
            ms = [jnp.full(prev.shape, -jnp.inf, F32) for _ in range(n_chain)]
            ids = [jnp.zeros(prev.shape, I32) for _ in range(n_chain)]
            for n in range(nk):
                sn = jnp.where(prev == n, -jnp.inf, s_ref[n])
                s_ref[n] = sn
                c = n % n_chain
                gt = sn > ms[c]
                ms[c] = jnp.where(gt, sn, ms[c])
                ids[c] = jnp.where(gt, n, ids[c])
            while len(ms) > 1:
                m_a, m_b, i_a, i_b = ms[0], ms[1], ids[0], ids[1]
                take_b = jnp.logical_or(m_b > m_a, jnp.logical_and(m_b == m_a, i_b < i_a))
                ms = ms[2:] + [jnp.where(take_b, m_b, m_a)]
                ids = ids[2:] + [jnp.where(take_b, i_b, i_a)]
            v_ref[side, r] = ms[0]
            i_ref[side, r] = ids[0]
            return ids[0]

        lax.fori_loop(0, PEER_TOPK, extract, jnp.full((PEER_HEADS, tt), -1, I32))

    v1 = [v_ref[0, r] for r in range(PEER_TOPK)]
    v2 = [v_ref[1, r] for r in range(PEER_TOPK)]
    i1 = [i_ref[0, r] for r in range(PEER_TOPK)]
    i2 = [i_ref[1, r] for r in range(PEER_TOPK)]
    cand = [v1[r1] + v2[r2] for r1, r2 in _PAIRS]
    flat = [r1 * PEER_TOPK + r2 for r1, r2 in _PAIRS]
    best, a_sel, b_sel = [], [], []
    for _ in range(PEER_TOPK):
        m = functools.reduce(jnp.maximum, cand)
        fsel = functools.reduce(jnp.minimum,
                                [jnp.where(cv == m, f, PEER_TOPK * PEER_TOPK) for cv, f in zip(cand, flat)])
        a = jnp.zeros_like(i1[0])
        b = jnp.zeros_like(i1[0])
        for c, ((r1, r2), f) in enumerate(zip(_PAIRS, flat)):
            hit = fsel == f
            a = jnp.where(hit, i1[r1], a)
            b = jnp.where(hit, i2[r2], b)
            cand[c] = jnp.where(hit, -jnp.inf, cand[c])
        best.append(m)
        a_sel.append(a)
        b_sel.append(b)
    ex = [jnp.exp(v - best[0]) for v in best]
    denom = functools.reduce(jnp.add, ex)
    gates = [e / denom for e in ex]
    nj = PEER_TOPK * PEER_HEADS
    a_ref[...] = jnp.stack(a_sel, axis=0).reshape(nj, tt).T
    b_ref[...] = jnp.stack(b_sel, axis=0).reshape(nj, tt).T
    g_ref[...] = jnp.stack(gates, axis=0).reshape(nj, tt).T


def _peer_topk(q, k1big, k2big, nt_rows):
    tt = LANES
    nj = PEER_TOPK * PEER_HEADS
    spec = pl.BlockSpec((tt, nj), lambda i: (i, 0))
    return pl.pallas_call(
        _peer_topk_kernel,
        grid=(nt_rows // tt,),
        in_specs=[
            pl.BlockSpec((tt, q.shape[1]), lambda i: (i, 0)),
            pl.BlockSpec(k1big.shape, lambda i: (0, 0)),
            pl.BlockSpec(k2big.shape, lambda i: (0, 0)),
        ],
        out_specs=[spec, spec, spec],
        out_shape=[jax.ShapeDtypeStruct((nt_rows, nj), I32), jax.ShapeDtypeStruct((nt_rows, nj), I32),
                   jax.ShapeDtypeStruct((nt_rows, nj), F32)],
        scratch_shapes=[pltpu.VMEM((PEER_NKEYS, PEER_HEADS, tt), F32),
                        pltpu.VMEM((2, PEER_TOPK, PEER_HEADS, tt), F32),
                        pltpu.VMEM((2, PEER_TOPK, PEER_HEADS, tt), I32)],
        compiler_params=_cparams(("parallel",), 32),
        name="peer_topk",
    )(q, k1big, k2big)


def _peer_gates_kernel(a_ref, b_ref, g_ref, o_ref, s_ref):
    tt, nj = a_ref.shape
    nk = PEER_NKEYS
    key = lax.broadcasted_iota(I32, (nk, nj), 0)

    def token(t, _):
        row = pl.ds(t, 1)
        left = jnp.where(key == a_ref[row, :], g_ref[row, :], 0.0).astype(BF16)
        right = jnp.where(key == b_ref[row, :], 1.0, 0.0).astype(BF16)
        s_ref[pl.ds(pl.multiple_of(t * nk, nk), nk), :] = lax.dot_general(
            left, right, NT_DIMS, preferred_element_type=F32)
        return 0

    lax.fori_loop(0, tt, token, 0, unroll=8)
    for i1 in range(nk):
        o_ref[:, i1 * nk:(i1 + 1) * nk] = s_ref[pl.ds(i1, tt, stride=nk), :].astype(o_ref.dtype)


def _peer_gates(a, b, g):
    rows, nj = a.shape
    tt = 64
    nk = PEER_NKEYS
    spec = pl.BlockSpec((tt, nj), lambda i: (i, 0))
    return pl.pallas_call(
        _peer_gates_kernel,
        grid=(rows // tt,),
        in_specs=[spec, spec, spec],
        out_specs=pl.BlockSpec((tt, nk * nk), lambda i: (i, 0)),
        out_shape=jax.ShapeDtypeStruct((rows, nk * nk), BF16),
        scratch_shapes=[pltpu.VMEM((tt * nk, nk), F32)],
        compiler_params=_cparams(("parallel",), 32),
        name="peer_gates",
    )(a, b, g)


def _peer_main_kernel(x_ref, sh_ref, sc_ref, gate_ref, lg_ref, lb_ref, u_ref, v_ref, gm_ref, o_ref,
                      h_ref, acc_ref):
    e = pl.program_id(1)

    @pl.when(e == 0)
    def _():
        h = _ln(x_ref[...]) * (1.0 + sc_ref[...]) + sh_ref[...]
        h_ref[...] = h.astype(BF16)
        acc_ref[...] = jnp.zeros_like(acc_ref)

    act = jax.nn.gelu(lax.dot_general(h_ref[...], u_ref[...], NT_DIMS, preferred_element_type=F32))
    w = (act * gm_ref[...].astype(F32)).astype(BF16)
    acc_ref[...] += jnp.dot(w, v_ref[...], preferred_element_type=F32)

    @pl.when(e == pl.num_programs(1) - 1)
    def _():
        z = ALPHA * x_ref[...] + gate_ref[...] * acc_ref[...]
        o_ref[...] = _ln(z) * lg_ref[...] + lb_ref[...]


def _peer_main(x, mods4, affine4, layer, u_tab, v_tab, gmat, nt, lat_rows):
    d = x.shape[1]
    n_exp = u_tab.shape[0]
    tm = TM
    te = 1024
    return pl.pallas_call(
        _peer_main_kernel,
        grid=(nt, n_exp // te),
        in_specs=[
            pl.BlockSpec((tm, d), lambda i, e: (i, 0)),
            _mod_spec(layer, 3, d, tm, lat_rows),
            _mod_spec(layer, 4, d, tm, lat_rows),
            _mod_spec(layer, 5, d, tm, lat_rows),
            _affine_spec(layer, 1, d),
            _affine_spec(layer, 1, d),
            pl.BlockSpec((te, d), lambda i, e: (e, 0)),
            pl.BlockSpec((te, d), lambda i, e: (e, 0)),
            pl.BlockSpec((tm, te), lambda i, e: (i, e)),
        ],
        out_specs=pl.BlockSpec((tm, d), lambda i, e: (i, 0)),
        out_shape=jax.ShapeDtypeStruct((nt * tm, d), F32),
        scratch_shapes=[pltpu.VMEM((tm, d), BF16), pltpu.VMEM((tm, d), F32)],
        compiler_params=_cparams(("parallel", "arbitrary"), 56),
        name="peer_main",
    )(x, mods4, mods4, mods4, affine4[0], affine4[1], u_tab, v_tab, gmat)


def _peer_ffn(x, mods4, affine4, layer, wq, k1big, k2big, u_tab, v_tab, nt, lat_rows):
    q = _lnmod_mm(x, mods4, layer, 3, wq, nt, lat_rows, BF16, tn=512)
    a, b, g = _peer_topk(q, k1big, k2big, nt * TM)
    gmat = _peer_gates(a, b, g)
    return _peer_main(x, mods4, affine4, layer, u_tab, v_tab, gmat, nt, lat_rows)


def _rope_tables(lat_len, batch, ctx_rows):
    rows = lat_len // GRID_W
    row = jnp.repeat(jnp.arange(rows, dtype=F32), GRID_W)
    col = jnp.tile(jnp.arange(GRID_W, dtype=F32), rows)
    nf = ROPE_DIM // 4
    inv = ROPE_BASE ** (-jnp.arange(nf, dtype=F32) / nf)
    ang_r = row[:, None] * inv
    ang_c = col[:, None] * inv
    cr, sr, cc, sc = jnp.cos(ang_r), jnp.sin(ang_r), jnp.cos(ang_c), jnp.sin(ang_c)
    cos64 = jnp.concatenate([cr, cr, cc, cc], axis=-1)
    sin64 = jnp.concatenate([-sr, sr, -sc, sc], axis=-1)
    cos = jnp.tile(jnp.concatenate([cos64, cos64], axis=-1), (batch, 1))
    sin = jnp.tile(jnp.concatenate([sin64, sin64], axis=-1), (batch, 1))
    cos = jnp.concatenate([cos, jnp.ones((ctx_rows, LANES), F32)], axis=0)
    sin = jnp.concatenate([sin, jnp.zeros((ctx_rows, LANES), F32)], axis=0)
    return cos, sin


def _expand_keys(k):
    nk, dk = k.shape
    eye = jnp.eye(PEER_HEADS, dtype=k.dtype)
    return jnp.einsum("nd,hg->nhgd", k, eye).reshape(nk * PEER_HEADS, PEER_HEADS * dk).astype(BF16)


def kernel(x, c, ctx, c_ctx, mod_w, mod_b, ln_g, ln_b, peer_wq, peer_k1, peer_k2, peer_u, peer_v,
           gm_w_in, gm_ln_g, gm_ln_b, gm_ws, gm_bs, gm_w_out,
           mla_w_in, mla_q_norm, mla_kv_norm, mla_w_uq, mla_w_ukv, mla_w_out,
           da_w_in, da_lambda, da_subln, da_w_out):
    batch, seq, d = x.shape
    ctx_len = ctx.shape[1]
    lat_rows = seq
    n_lat = batch * seq
    n_all = n_lat + batch * ctx_len
    nt_lat = n_lat // TM
    nt_all = n_all // TM

    xs = jnp.concatenate([x.reshape(n_lat, d), ctx.reshape(batch * ctx_len, d)], axis=0)
    c_all = jnp.concatenate([c, c_ctx[None], jnp.zeros((8 - batch - 1, d), F32)], axis=0)
    mods4 = _mods(c_all, mod_w, mod_b).reshape(DEPTH, 8, 1, 6 * d)
    rope_tabs = _rope_tables(seq, batch, batch * ctx_len)

    for i in range(DEPTH):
        kind = i % 3
        j = i // 3
        last = i == DEPTH - 1
        nt = nt_lat if last else nt_all
        affine4 = (ln_g.reshape(DEPTH, 2, 1, d), ln_b.reshape(DEPTH, 2, 1, d))
        if kind == 0:
            z = _lnmod_mm(xs, mods4, i, 0, gm_w_in[j].astype(BF16), nt, lat_rows, F32, tn=512,
                          epilogue="gelu")
            groups = gm_ws.shape[1]
            bsb = jnp.broadcast_to(gm_bs[j][:, :, None], (groups, GM_CHUNK, LANES))
            xs_new = _gmlp_out(z, gm_ln_g[j][None], gm_ln_b[j][None], gm_ws[j].astype(BF16), bsb,
                               gm_w_out[j].astype(BF16), xs, mods4, affine4, i, nt, lat_rows)
        elif kind == 1:
            heads = mla_w_out.shape[1] // LANES
            qr = mla_q_norm.shape[1]
            kvr = mla_kv_norm.shape[1]
            w_in = jnp.pad(mla_w_in[j], ((0, 0), (0, LANES - MLA_ROPE))).astype(BF16)
            hd = _lnmod_mm(xs, mods4, i, 0, w_in, nt, lat_rows, F32, tn=w_in.shape[1] // 3)
            wq = mla_w_uq[j].reshape(qr, heads, MLA_NOPE + MLA_ROPE)
            wq = jnp.pad(wq, ((0, 0), (0, 0), (0, 2 * LANES - MLA_NOPE - MLA_ROPE)))
            wq = wq.reshape(qr, heads * 2 * LANES).astype(BF16)
            wkv = mla_w_ukv[j].reshape(kvr, heads, 2 * LANES)
            wk = wkv[:, :, :MLA_NOPE].reshape(kvr, heads * LANES).astype(BF16)
            wv = wkv[:, :, MLA_NOPE:].reshape(kvr, heads * LANES).astype(BF16)
            scale = (MLA_NOPE + MLA_ROPE) ** -0.5 * LOG2E
            qm = _rms_mm(hd, 0, mla_q_norm[j][None], wq, nt, "mla_q", 2 * LANES, 2 * LANES,
                         rope_tabs=rope_tabs, q_scale=scale)
            km = _rms_mm(hd, 1, mla_kv_norm[j][None], wk, nt, "mla_k", LANES, 2 * LANES,
                         rope_tabs=rope_tabs, kr_block=(qr + kvr) // LANES)
            vm = _rms_mm(hd, 1, mla_kv_norm[j][None], wv, nt, "plain", 512, 512)
            o = _attention(qm, 0, km, 0, vm, 0, 2 * LANES, LANES, heads, batch, seq, ctx_len, 1)
            xs_new = _mm_postnorm(o, mla_w_out[j].astype(BF16), xs, mods4, affine4, i, 2, nt, lat_rows)
        else:
            heads = d // (2 * DA_HEAD)
            lam_init = 0.8 - 0.6 * math.exp(-0.3 * i)
            qkv = _lnmod_mm(xs, mods4, i, 0, da_w_in[j].astype(BF16), nt, lat_rows, BF16, tn=512,
                            epilogue="rope", rope_tabs=rope_tabs, n_rope_tiles=2 * d // 512,
                            q_tiles=d // 512, q_scale=DA_HEAD ** -0.5 * LOG2E)
            o = _attention(qkv, 0, qkv, heads, qkv, 2 * heads, LANES, LANES, heads, batch, seq, ctx_len, 2,
                           extra=(da_lambda[j], da_subln[j][None]), lam_init=lam_init)
            xs_new = _mm_postnorm(o, da_w_out[j].astype(BF16), xs, mods4, affine4, i, 2, nt, lat_rows)
        xs = xs_new
        wq_p = peer_wq[i].reshape(d, PEER_HEADS, 2, -1).transpose(0, 2, 1, 3).reshape(d, -1).astype(BF16)
        xs = _peer_ffn(xs, mods4, affine4, i, wq_p, _expand_keys(peer_k1[i]), _expand_keys(peer_k2[i]),
                       peer_u[i].astype(BF16), peer_v[i].astype(BF16), nt, lat_rows)
    return xs[:n_lat].reshape(batch, seq, d)
```

```python
import functools
import math

import jax
import jax.numpy as jnp
from jax import lax
from jax.experimental import pallas as pl
from jax.experimental.pallas import tpu as pltpu

F32 = jnp.float32
BF16 = jnp.bfloat16
I32 = jnp.int32

LN_EPS = 1e-6
DEPTH = 4
ALPHA = (2.0 * DEPTH) ** 0.25
GRID_W = 64
ROPE_DIM = 64
ROPE_BASE = 10000.0
GM_CHUNK = 128
MLA_NOPE = 128
MLA_ROPE = 64
DA_HEAD = 64
PEER_HEADS = 8
PEER_TOPK = 16
PEER_NKEYS = 128

LANES = 128
TM = 512
MIB = 1024 * 1024

NT_DIMS = (((1,), (1,)), ((), ()))
LOG2E = math.log2(math.e)


def _cparams(sem, vmem_mib):
    return pltpu.CompilerParams(dimension_semantics=sem, vmem_limit_bytes=vmem_mib * MIB)


def _ln(xf):
    mu = jnp.mean(xf, axis=-1, keepdims=True)
    var = jnp.mean(jnp.square(xf - mu), axis=-1, keepdims=True)
    return (xf - mu) * lax.rsqrt(var + LN_EPS)


def _rms(xf):
    return xf * lax.rsqrt(jnp.mean(jnp.square(xf), axis=-1, keepdims=True) + LN_EPS)


def _swap16(x):
    n = x.shape[-1]
    lane = lax.broadcasted_iota(I32, x.shape, x.ndim - 1)
    up = pltpu.roll(x, n - 16, x.ndim - 1)
    dn = pltpu.roll(x, 16, x.ndim - 1)
    return jnp.where((lane & 16) == 0, up, dn)


def _rope(x, cos, sin):
    reps = x.shape[-1] // LANES
    if reps > 1:
        cos = jnp.concatenate([cos] * reps, axis=-1)
        sin = jnp.concatenate([sin] * reps, axis=-1)
    return x * cos + _swap16(x) * sin


def _mod_kernel(c_ref, w_ref, b_ref, o_ref):
    c = c_ref[...]
    s = c * jax.nn.sigmoid(c)
    o_ref[...] = jnp.dot(s.astype(BF16), w_ref[...].astype(BF16),
                         preferred_element_type=F32) + b_ref[...]


def _mods(c_all, mod_w, mod_b):
    depth, d, n = mod_w.shape
    tn = 1024
    return pl.pallas_call(
        _mod_kernel,
        grid=(depth, n // tn),
        in_specs=[
            pl.BlockSpec((8, d), lambda l, j: (0, 0)),
            pl.BlockSpec((None, d, tn), lambda l, j: (l, 0, j)),
            pl.BlockSpec((None, 1, tn), lambda l, j: (l, 0, j)),
        ],
        out_specs=pl.BlockSpec((None, 8, tn), lambda l, j: (l, 0, j)),
        out_shape=jax.ShapeDtypeStruct((depth, 8, n), F32),
        compiler_params=_cparams(("parallel", "arbitrary"), 40),
        name="mods",
    )(c_all, mod_w, mod_b.reshape(depth, 1, n))


def _grp(i, tm, lat_rows):
    return jnp.minimum((i * tm) // lat_rows, 2)


def _mod_spec(layer, k, d, tm, lat_rows):
    return pl.BlockSpec((None, None, 1, d), lambda i, *_: (layer, _grp(i, tm, lat_rows), 0, k))


def _affine_spec(layer, which, d):
    return pl.BlockSpec((None, None, 1, d), lambda i, *_: (layer, which, 0, 0))


def _lnmod_mm_kernel(*refs, epilogue, n_rope_tiles, q_tiles, q_scale):
    if epilogue == "rope":
        x_ref, sh_ref, sc_ref, w_ref, cos_ref, sin_ref, o_ref, h_ref = refs
    else:
        x_ref, sh_ref, sc_ref, w_ref, o_ref, h_ref = refs
    j = pl.program_id(1)

    @pl.when(j == 0)
    def _():
        h = _ln(x_ref[...]) * (1.0 + sc_ref[...]) + sh_ref[...]
        h_ref[...] = h.astype(BF16)

    acc = jnp.dot(h_ref[...], w_ref[...], preferred_element_type=F32)
    if epilogue == "gelu":
        o_ref[...] = jax.nn.gelu(acc).astype(o_ref.dtype)
    elif epilogue == "rope":
        @pl.when(j < q_tiles)
        def _():
            o_ref[...] = (_rope(acc, cos_ref[...], sin_ref[...]) * q_scale).astype(o_ref.dtype)

        @pl.when(jnp.logical_and(j >= q_tiles, j < n_rope_tiles))
        def _():
            o_ref[...] = _rope(acc, cos_ref[...], sin_ref[...]).astype(o_ref.dtype)

        @pl.when(j >= n_rope_tiles)
        def _():
            o_ref[...] = acc.astype(o_ref.dtype)
    else:
        o_ref[...] = acc.astype(o_ref.dtype)


def _lnmod_mm(x, mods4, layer, k_shift, w, nt, lat_rows, out_dtype, tn, epilogue="none",
              rope_tabs=None, n_rope_tiles=0, q_tiles=0, q_scale=1.0):
    d = x.shape[1]
    n = w.shape[1]
    tm = TM
    in_specs = [
        pl.BlockSpec((tm, d), lambda i, j: (i, 0)),
        _mod_spec(layer, k_shift, d, tm, lat_rows),
        _mod_spec(layer, k_shift + 1, d, tm, lat_rows),
        pl.BlockSpec((d, tn), lambda i, j: (0, j)),
    ]
    args = [x, mods4, mods4, w]
    if epilogue == "rope":
        in_specs += [pl.BlockSpec((tm, LANES), lambda i, j: (i, 0))] * 2
        args += list(rope_tabs)
    return pl.pallas_call(
        functools.partial(_lnmod_mm_kernel, epilogue=epilogue, n_rope_tiles=n_rope_tiles,
                          q_tiles=q_tiles, q_scale=q_scale),
        grid=(nt, n // tn),
        in_specs=in_specs,
        out_specs=pl.BlockSpec((tm, tn), lambda i, j: (i, j)),
        out_shape=jax.ShapeDtypeStruct((nt * tm, n), out_dtype),
        scratch_shapes=[pltpu.VMEM((tm, d), BF16)],
        compiler_params=_cparams(("parallel", "arbitrary"), 40),
        name="lnmod_mm_" + epilogue,
    )(*args)


def _mm_postnorm_kernel(a_ref, w_ref, x_ref, gate_ref, g_ref, b_ref, o_ref):
    y = jnp.dot(a_ref[...], w_ref[...], preferred_element_type=F32)
    z = ALPHA * x_ref[...] + gate_ref[...] * y
    o_ref[...] = _ln(z) * g_ref[...] + b_ref[...]


def _mm_postnorm(a, w, x, mods4, affine4, layer, k_gate, nt, lat_rows):
    d = x.shape[1]
    kdim = a.shape[1]
    tm = TM
    return pl.pallas_call(
        _mm_postnorm_kernel,
        grid=(nt,),
        in_specs=[
            pl.BlockSpec((tm, kdim), lambda i: (i, 0)),
            pl.BlockSpec((kdim, d), lambda i: (0, 0)),
            pl.BlockSpec((tm, d), lambda i: (i, 0)),
            _mod_spec(layer, k_gate, d, tm, lat_rows),
            _affine_spec(layer, 0, d),
            _affine_spec(layer, 0, d),
        ],
        out_specs=pl.BlockSpec((tm, d), lambda i: (i, 0)),
        out_shape=jax.ShapeDtypeStruct((nt * tm, d), F32),
        compiler_params=_cparams(("parallel",), 48),
        name="mm_postnorm",
    )(a, w, x, mods4, affine4[0], affine4[1])


def _gmlp_out_kernel(u_ref, v_ref, lg_ref, lb_ref, ws_ref, bs_ref, w_ref, x_ref, gate_ref,
                     g_ref, b_ref, o_ref, vn_ref, t_ref):
    tm, gd = u_ref.shape
    groups = gd // LANES
    vn_ref[...] = (_ln(v_ref[...].astype(F32)) * lg_ref[...] + lb_ref[...]).astype(BF16)
    for c in range(tm // GM_CHUNK):
        rows = pl.ds(c * GM_CHUNK, GM_CHUNK)
        for g in range(groups):
            cols = pl.ds(g * LANES, LANES)
            sv = jnp.dot(ws_ref[g], vn_ref[rows, cols], preferred_element_type=F32) + bs_ref[g]
            t_ref[rows, cols] = (u_ref[rows, cols].astype(F32) * sv).astype(BF16)
    y = jnp.dot(t_ref[...], w_ref[...], preferred_element_type=F32)
    z = ALPHA * x_ref[...] + gate_ref[...] * y
    o_ref[...] = _ln(z) * g_ref[...] + b_ref[...]


def _gmlp_out(z, ln_g, ln_b, ws, bsb, w_out, x, mods4, affine4, layer, nt, lat_rows):
    d = x.shape[1]
    gd = z.shape[1] // 2
    groups = gd // LANES
    tm = TM // 2
    nt = nt * 2
    return pl.pallas_call(
        _gmlp_out_kernel,
        grid=(nt,),
        in_specs=[
            pl.BlockSpec((tm, gd), lambda i: (i, 0)),
            pl.BlockSpec((tm, gd), lambda i: (i, 1)),
            pl.BlockSpec((1, gd), lambda i: (0, 0)),
            pl.BlockSpec((1, gd), lambda i: (0, 0)),
            pl.BlockSpec((groups, GM_CHUNK, GM_CHUNK), lambda i: (0, 0, 0)),
            pl.BlockSpec((groups, GM_CHUNK, LANES), lambda i: (0, 0, 0)),
            pl.BlockSpec((gd, d), lambda i: (0, 0)),
            pl.BlockSpec((tm, d), lambda i: (i, 0)),
            _mod_spec(layer, 2, d, tm, lat_rows),
            _affine_spec(layer, 0, d),
            _affine_spec(layer, 0, d),
        ],
        out_specs=pl.BlockSpec((tm, d), lambda i: (i, 0)),
        out_shape=jax.ShapeDtypeStruct((nt * tm, d), F32),
        scratch_shapes=[pltpu.VMEM((tm, gd), BF16), pltpu.VMEM((tm, gd), BF16)],
        compiler_params=_cparams(("parallel",), 56),
        name="gmlp_out",
    )(z, z, ln_g, ln_b, ws, bsb, w_out, x, mods4, affine4[0], affine4[1])


def _rms_mm_kernel(*refs, mode, q_scale):
    if mode == "mla_q":
        c_ref, nw_ref, w_ref, cos_ref, sin_ref, o_ref, cn_ref = refs
    elif mode == "mla_k":
        c_ref, nw_ref, w_ref, kr_ref, cos_ref, sin_ref, o_ref, cn_ref = refs
    else:
        c_ref, nw_ref, w_ref, o_ref, cn_ref = refs
    j = pl.program_id(1)

    @pl.when(j == 0)
    def _():
        cn_ref[...] = (_rms(c_ref[...]) * nw_ref[...]).astype(BF16)

    acc = jnp.dot(cn_ref[...], w_ref[...], preferred_element_type=F32)
    if mode == "mla_q":
        roped = _rope(acc[:, LANES:], cos_ref[...], sin_ref[...])
        o_ref[...] = (jnp.concatenate([acc[:, :LANES], roped], axis=-1) * q_scale).astype(o_ref.dtype)
    elif mode == "mla_k":
        roped = _rope(kr_ref[...], cos_ref[...], sin_ref[...])
        o_ref[...] = jnp.concatenate([acc, roped], axis=-1).astype(o_ref.dtype)
    else:
        o_ref[...] = acc.astype(o_ref.dtype)


def _rms_mm(hd, col_block, norm_w, w, nt, mode, tn_w, tn_o, rope_tabs=None, kr_block=None, q_scale=1.0):
    rank = norm_w.shape[1]
    n_w = w.shape[1]
    tm = TM
    steps = n_w // tn_w
    in_specs = [
        pl.BlockSpec((tm, rank), lambda i, j: (i, col_block)),
        pl.BlockSpec((1, rank), lambda i, j: (0, 0)),
        pl.BlockSpec((rank, tn_w), lambda i, j: (0, j)),
    ]
    args = [hd, norm_w, w]
    if mode == "mla_k":
        in_specs.append(pl.BlockSpec((tm, LANES), lambda i, j: (i, kr_block)))
        args.append(hd)
    if mode in ("mla_q", "mla_k"):
        in_specs += [pl.BlockSpec((tm, LANES), lambda i, j: (i, 0))] * 2
        args += list(rope_tabs)
    return pl.pallas_call(
        functools.partial(_rms_mm_kernel, mode=mode, q_scale=q_scale),
        grid=(nt, steps),
        in_specs=in_specs,
        out_specs=pl.BlockSpec((tm, tn_o), lambda i, j: (i, j)),
        out_shape=jax.ShapeDtypeStruct((nt * tm, steps * tn_o), BF16),
        scratch_shapes=[pltpu.VMEM((tm, rank), BF16)],
        compiler_params=_cparams(("parallel", "arbitrary"), 32),
        name="rms_mm_" + mode,
    )(*args)


def _attn_kernel(*refs, n_streams, has_lat, ck, lam_init):
    refs = list(refs)
    o_ref = refs.pop()
    q_ref = refs.pop(0)
    if has_lat:
        kl_ref, kc_ref, vl_ref, vc_ref = refs[:4]
        refs = refs[4:]
    else:
        kc_ref, vc_ref = refs[:2]
        refs = refs[2:]
    tq, dk = q_ref.shape
    dv = vc_ref.shape[1]
    q = q_ref[...]
    if n_streams == 2:
        lane = lax.broadcasted_iota(I32, (tq, dk), 1)
        zero = jnp.zeros_like(q)
        q = jnp.concatenate([jnp.where(lane < dk // 2, q, zero), jnp.where(lane >= dk // 2, q, zero)], axis=0)

    def chunk(carry, k, v):
        s = lax.dot_general(q, k, NT_DIMS, preferred_element_type=F32)
        out = []
        for t in range(n_streams):
            m, l, acc = carry[3 * t:3 * t + 3]
            st = s[t * tq:(t + 1) * tq]
            m_new = jnp.maximum(m, jnp.max(st, axis=-1, keepdims=True))
            alpha = jnp.exp2(m - m_new)
            e = jnp.exp2(st - m_new)
            l = alpha * l + jnp.sum(e, axis=-1, keepdims=True)
            acc = alpha * acc + jnp.dot(e.astype(BF16), v, preferred_element_type=F32)
            out += [m_new, l, acc]
        return tuple(out)

    init = []
    for _ in range(n_streams):
        init += [jnp.full((tq, 1), -jnp.inf, F32), jnp.zeros((tq, 1), F32), jnp.zeros((tq, dv), F32)]
    carry = chunk(tuple(init), kc_ref[...], vc_ref[...])
    if has_lat:
        for c in range(kl_ref.shape[0] // ck):
            carry = chunk(carry, kl_ref[c * ck:(c + 1) * ck, :], vl_ref[c * ck:(c + 1) * ck, :])

    if n_streams == 1:
        _, l, acc = carry
        o_ref[...] = (acc / l).astype(o_ref.dtype)
    else:
        lam_ref, sub_ref = refs
        _, l1, a1, _, l2, a2 = carry
        lp = lam_ref[...].astype(F32)
        lam = (jnp.exp(jnp.sum(lp[0:1] * lp[1:2], axis=-1, keepdims=True))
               - jnp.exp(jnp.sum(lp[2:3] * lp[3:4], axis=-1, keepdims=True)) + lam_init)
        o = a1 / l1 - lam * (a2 / l2)
        o_ref[...] = (_rms(o) * sub_ref[...] * (1.0 - lam_init)).astype(o_ref.dtype)


def _attention(q_arr, q_cb, k_arr, k_cb, v_arr, v_cb, dk, dv, heads, batch, lat_len, ctx_len,
               n_streams, extra=(), lam_init=0.0):
    ctx_block0 = batch * lat_len // ctx_len
    outs = []
    for has_lat in (True, False):
        tq = (1024 if n_streams == 1 else 512) if has_lat else ctx_len
        n_q = lat_len // tq if has_lat else 1
        q_block0 = 0 if has_lat else ctx_block0

        def q_map(b, h, qi, n_q=n_q, q_block0=q_block0):
            return (q_block0 + b * n_q + qi, q_cb + h)

        kc_spec = pl.BlockSpec((ctx_len, dk), lambda b, h, qi: (ctx_block0 + b, k_cb + h))
        vc_spec = pl.BlockSpec((ctx_len, dv), lambda b, h, qi: (ctx_block0 + b, v_cb + h))
        in_specs = [pl.BlockSpec((tq, dk), q_map)]
        args = [q_arr]
        if has_lat:
            in_specs += [pl.BlockSpec((lat_len, dk), lambda b, h, qi: (b, k_cb + h)), kc_spec,
                         pl.BlockSpec((lat_len, dv), lambda b, h, qi: (b, v_cb + h)), vc_spec]
            args += [k_arr, k_arr, v_arr, v_arr]
        else:
            in_specs += [kc_spec, vc_spec]
            args += [k_arr, v_arr]
        for e in extra:
            in_specs.append(pl.BlockSpec(e.shape, lambda b, h, qi: (0, 0)))
            args.append(e)
        outs.append(pl.pallas_call(
            functools.partial(_attn_kernel, n_streams=n_streams, has_lat=has_lat, ck=1024, lam_init=lam_init),
            grid=(batch, heads, n_q),
            in_specs=in_specs,
            out_specs=pl.BlockSpec((tq, dv), lambda b, h, qi, n_q=n_q: (b * n_q + qi, h)),
            out_shape=jax.ShapeDtypeStruct((batch * n_q * tq, heads * dv), BF16),
            compiler_params=_cparams(("parallel", "parallel", "arbitrary"), 56),
            name="attention_%d_%s" % (n_streams, "lat" if has_lat else "ctx"),
        )(*args))
    return jnp.concatenate(outs, axis=0)


_PAIRS = [(r1, r2) for r1 in range(PEER_TOPK) for r2 in range(PEER_TOPK)
          if (r1 + 1) * (r2 + 1) <= PEER_TOPK]


def _peer_topk_kernel(q_ref, k1_ref, k2_ref, a_ref, b_ref, g_ref, s_ref, v_ref, i_ref):
    tt = q_ref.shape[0]
    half = q_ref.shape[1] // 2
    nk = PEER_NKEYS
    n_chain = 4

    for side, k_ref in enumerate((k1_ref, k2_ref)):
        qs = q_ref[:, side * half:(side + 1) * half]
        s = lax.dot_general(k_ref[...], qs, NT_DIMS, preferred_element_type=F32)
        s_ref[...] = s.reshape(nk, PEER_HEADS, tt)

        def extract(r, prev):
            ms = [jnp.full(prev.shape, -jnp.inf, F32) for _ in range(n_chain)]
            ids = [jnp.zeros(prev.shape, I32) for _ in range(n_chain)]
            for n in range(nk):
                sn = jnp.where(prev == n, -jnp.inf, s_ref[n])
                s_ref[n] = sn
                c = n % n_chain
                gt = sn > ms[c]
                ms[c] = jnp.where(gt, sn, ms[c])
                ids[c] = jnp.where(gt, n, ids[c])
            while len(ms) > 1:
                m_a, m_b, i_a, i_b = ms[0], ms[1], ids[0], ids[1]
                take_b = jnp.logical_or(m_b > m_a, jnp.logical_and(m_b == m_a, i_b < i_a))
                ms = ms[2:] + [jnp.where(take_b, m_b, m_a)]
                ids = ids[2:] + [jnp.where(take_b, i_b, i_a)]
            v_ref[side, r] = ms[0]
            i_ref[side, r] = ids[0]
            return ids[0]

        lax.fori_loop(0, PEER_TOPK, extract, jnp.full((PEER_HEADS, tt), -1, I32))

    v1 = [v_ref[0, r] for r in range(PEER_TOPK)]
    v2 = [v_ref[1, r] for r in range(PEER_TOPK)]
    i1 = [i_ref[0, r] for r in range(PEER_TOPK)]
    i2 = [i_ref[1, r] for r in range(PEER_TOPK)]
    cand = [v1[r1] + v2[r2] for r1, r2 in _PAIRS]
    flat = [r1 * PEER_TOPK + r2 for r1, r2 in _PAIRS]
    best, a_sel, b_sel = [], [], []
    for _ in range(PEER_TOPK):
        m = functools.reduce(jnp.maximum, cand)
        fsel = functools.reduce(jnp.minimum,
                                [jnp.where(cv == m, f, PEER_TOPK * PEER_TOPK) for cv, f in zip(cand, flat)])
        a = jnp.zeros_like(i1[0])
        b = jnp.zeros_like(i1[0])
        for c, ((r1, r2), f) in enumerate(zip(_PAIRS, flat)):
            hit = fsel == f
            a = jnp.where(hit, i1[r1], a)
            b = jnp.where(hit, i2[r2], b)
            cand[c] = jnp.where(hit, -jnp.inf, cand[c])
        best.append(m)
        a_sel.append(a)
        b_sel.append(b)
    ex = [jnp.exp(v - best[0]) for v in best]
    denom = functools.reduce(jnp.add, ex)
    gates = [e / denom for e in ex]
    nj = PEER_TOPK * PEER_HEADS
    a_ref[...] = jnp.stack(a_sel, axis=0).reshape(nj, tt).T
    b_ref[...] = jnp.stack(b_sel, axis=0).reshape(nj, tt).T
    g_ref[...] = jnp.stack(gates, axis=0).reshape(nj, tt).T


def _peer_topk(q, k1big, k2big, nt_rows):
    tt = 2 * LANES
    nj = PEER_TOPK * PEER_HEADS
    spec = pl.BlockSpec((tt, nj), lambda i: (i, 0))
    return pl.pallas_call(
        _peer_topk_kernel,
        grid=(nt_rows // tt,),
        in_specs=[
            pl.BlockSpec((tt, q.shape[1]), lambda i: (i, 0)),
            pl.BlockSpec(k1big.shape, lambda i: (0, 0)),
            pl.BlockSpec(k2big.shape, lambda i: (0, 0)),
        ],
        out_specs=[spec, spec, spec],
        out_shape=[jax.ShapeDtypeStruct((nt_rows, nj), I32), jax.ShapeDtypeStruct((nt_rows, nj), I32),
                   jax.ShapeDtypeStruct((nt_rows, nj), F32)],
        scratch_shapes=[pltpu.VMEM((PEER_NKEYS, PEER_HEADS, tt), F32),
                        pltpu.VMEM((2, PEER_TOPK, PEER_HEADS, tt), F32),
                        pltpu.VMEM((2, PEER_TOPK, PEER_HEADS, tt), I32)],
        compiler_params=_cparams(("parallel",), 32),
        name="peer_topk",
    )(q, k1big, k2big)


def _peer_gates_kernel(a_ref, b_ref, g_ref, o_ref, s_ref):
    tt, nj = a_ref.shape
    nk = PEER_NKEYS
    key = lax.broadcasted_iota(I32, (nk, nj), 0)

    def token(t, _):
        row = pl.ds(t, 1)
        left = jnp.where(key == a_ref[row, :], g_ref[row, :], 0.0).astype(BF16)
        right = jnp.where(key == b_ref[row, :], 1.0, 0.0).astype(BF16)
        s_ref[pl.ds(pl.multiple_of(t * nk, nk), nk), :] = lax.dot_general(
            left, right, NT_DIMS, preferred_element_type=F32)
        return 0

    lax.fori_loop(0, tt, token, 0, unroll=8)
    for i1 in range(nk):
        o_ref[:, i1 * nk:(i1 + 1) * nk] = s_ref[pl.ds(i1, tt, stride=nk), :].astype(o_ref.dtype)


def _peer_gates(a, b, g):
    rows, nj = a.shape
    tt = 64
    nk = PEER_NKEYS
    spec = pl.BlockSpec((tt, nj), lambda i: (i, 0))
    return pl.pallas_call(
        _peer_gates_kernel,
        grid=(rows // tt,),
        in_specs=[spec, spec, spec],
        out_specs=pl.BlockSpec((tt, nk * nk), lambda i: (i, 0)),
        out_shape=jax.ShapeDtypeStruct((rows, nk * nk), BF16),
        scratch_shapes=[pltpu.VMEM((tt * nk, nk), F32)],
        compiler_params=_cparams(("parallel",), 32),
        name="peer_gates",
    )(a, b, g)


def _peer_main_kernel(x_ref, sh_ref, sc_ref, gate_ref, lg_ref, lb_ref, u_ref, v_ref, gm_ref, o_ref,
                      h_ref, acc_ref):
    e = pl.program_id(1)

    @pl.when(e == 0)
    def _():
        h = _ln(x_ref[...]) * (1.0 + sc_ref[...]) + sh_ref[...]
        h_ref[...] = h.astype(BF16)
        acc_ref[...] = jnp.zeros_like(acc_ref)

    a = lax.dot_general(h_ref[...], u_ref[...], NT_DIMS, preferred_element_type=F32)
    w = jax.nn.gelu(a.astype(BF16)) * gm_ref[...]
    acc_ref[...] += jnp.dot(w, v_ref[...], preferred_element_type=F32)

    @pl.when(e == pl.num_programs(1) - 1)
    def _():
        z = ALPHA * x_ref[...] + gate_ref[...] * acc_ref[...]
        o_ref[...] = _ln(z) * lg_ref[...] + lb_ref[...]


def _peer_main(x, mods4, affine4, layer, u_tab, v_tab, gmat, nt, lat_rows):
    d = x.shape[1]
    n_exp = u_tab.shape[0]
    tm = TM
    te = 1024
    return pl.pallas_call(
        _peer_main_kernel,
        grid=(nt, n_exp // te),
        in_specs=[
            pl.BlockSpec((tm, d), lambda i, e: (i, 0)),
            _mod_spec(layer, 3, d, tm, lat_rows),
            _mod_spec(layer, 4, d, tm, lat_rows),
            _mod_spec(layer, 5, d, tm, lat_rows),
            _affine_spec(layer, 1, d),
            _affine_spec(layer, 1, d),
            pl.BlockSpec((te, d), lambda i, e: (e, 0)),
            pl.BlockSpec((te, d), lambda i, e: (e, 0)),
            pl.BlockSpec((tm, te), lambda i, e: (i, e)),
        ],
        out_specs=pl.BlockSpec((tm, d), lambda i, e: (i, 0)),
        out_shape=jax.ShapeDtypeStruct((nt * tm, d), F32),
        scratch_shapes=[pltpu.VMEM((tm, d), BF16), pltpu.VMEM((tm, d), F32)],
        compiler_params=_cparams(("parallel", "arbitrary"), 56),
        name="peer_main",
    )(x, mods4, mods4, mods4, affine4[0], affine4[1], u_tab, v_tab, gmat)


def _peer_ffn(x, mods4, affine4, layer, wq, k1big, k2big, u_tab, v_tab, nt, lat_rows):
    q = _lnmod_mm(x, mods4, layer, 3, wq, nt, lat_rows, BF16, tn=512)
    a, b, g = _peer_topk(q, k1big, k2big, nt * TM)
    gmat = _peer_gates(a, b, g)
    return _peer_main(x, mods4, affine4, layer, u_tab, v_tab, gmat, nt, lat_rows)


def _rope_tables(lat_len, batch, ctx_rows):
    rows = lat_len // GRID_W
    row = jnp.repeat(jnp.arange(rows, dtype=F32), GRID_W)
    col = jnp.tile(jnp.arange(GRID_W, dtype=F32), rows)
    nf = ROPE_DIM // 4
    inv = ROPE_BASE ** (-jnp.arange(nf, dtype=F32) / nf)
    ang_r = row[:, None] * inv
    ang_c = col[:, None] * inv
    cr, sr, cc, sc = jnp.cos(ang_r), jnp.sin(ang_r), jnp.cos(ang_c), jnp.sin(ang_c)
    cos64 = jnp.concatenate([cr, cr, cc, cc], axis=-1)
    sin64 = jnp.concatenate([-sr, sr, -sc, sc], axis=-1)
    cos = jnp.tile(jnp.concatenate([cos64, cos64], axis=-1), (batch, 1))
    sin = jnp.tile(jnp.concatenate([sin64, sin64], axis=-1), (batch, 1))
    cos = jnp.concatenate([cos, jnp.ones((ctx_rows, LANES), F32)], axis=0)
    sin = jnp.concatenate([sin, jnp.zeros((ctx_rows, LANES), F32)], axis=0)
    return cos, sin


def _expand_keys(k):
    nk, dk = k.shape
    eye = jnp.eye(PEER_HEADS, dtype=k.dtype)
    return jnp.einsum("nd,hg->nhgd", k, eye).reshape(nk * PEER_HEADS, PEER_HEADS * dk).astype(BF16)


def kernel(x, c, ctx, c_ctx, mod_w, mod_b, ln_g, ln_b, peer_wq, peer_k1, peer_k2, peer_u, peer_v,
           gm_w_in, gm_ln_g, gm_ln_b, gm_ws, gm_bs, gm_w_out,
           mla_w_in, mla_q_norm, mla_kv_norm, mla_w_uq, mla_w_ukv, mla_w_out,
           da_w_in, da_lambda, da_subln, da_w_out):
    batch, seq, d = x.shape
    ctx_len = ctx.shape[1]
    lat_rows = seq
    n_lat = batch * seq
    n_all = n_lat + batch * ctx_len
    nt_lat = n_lat // TM
    nt_all = n_all // TM

    xs = jnp.concatenate([x.reshape(n_lat, d), ctx.reshape(batch * ctx_len, d)], axis=0)
    c_all = jnp.concatenate([c, c_ctx[None], jnp.zeros((8 - batch - 1, d), F32)], axis=0)
    mods4 = _mods(c_all, mod_w, mod_b).reshape(DEPTH, 8, 1, 6 * d)
    rope_tabs = _rope_tables(seq, batch, batch * ctx_len)

    for i in range(DEPTH):
        kind = i % 3
        j = i // 3
        last = i == DEPTH - 1
        nt = nt_lat if last else nt_all
        affine4 = (ln_g.reshape(DEPTH, 2, 1, d), ln_b.reshape(DEPTH, 2, 1, d))
        if kind == 0:
            z = _lnmod_mm(xs, mods4, i, 0, gm_w_in[j].astype(BF16), nt, lat_rows, F32, tn=512,
                          epilogue="gelu")
            groups = gm_ws.shape[1]
            bsb = jnp.broadcast_to(gm_bs[j][:, :, None], (groups, GM_CHUNK, LANES))
            xs_new = _gmlp_out(z, gm_ln_g[j][None], gm_ln_b[j][None], gm_ws[j].astype(BF16), bsb,
                               gm_w_out[j].astype(BF16), xs, mods4, affine4, i, nt, lat_rows)
        elif kind == 1:
            heads = mla_w_out.shape[1] // LANES
            qr = mla_q_norm.shape[1]
            kvr = mla_kv_norm.shape[1]
            w_in = jnp.pad(mla_w_in[j], ((0, 0), (0, LANES - MLA_ROPE))).astype(BF16)
            hd = _lnmod_mm(xs, mods4, i, 0, w_in, nt, lat_rows, F32, tn=w_in.shape[1] // 3)
            wq = mla_w_uq[j].reshape(qr, heads, MLA_NOPE + MLA_ROPE)
            wq = jnp.pad(wq, ((0, 0), (0, 0), (0, 2 * LANES - MLA_NOPE - MLA_ROPE)))
            wq = wq.reshape(qr, heads * 2 * LANES).astype(BF16)
            wkv = mla_w_ukv[j].reshape(kvr, heads, 2 * LANES)
            wk = wkv[:, :, :MLA_NOPE].reshape(kvr, heads * LANES).astype(BF16)
            wv = wkv[:, :, MLA_NOPE:].reshape(kvr, heads * LANES).astype(BF16)
            scale = (MLA_NOPE + MLA_ROPE) ** -0.5 * LOG2E
            qm = _rms_mm(hd, 0, mla_q_norm[j][None], wq, nt, "mla_q", 2 * LANES, 2 * LANES,
                         rope_tabs=rope_tabs, q_scale=scale)
            km = _rms_mm(hd, 1, mla_kv_norm[j][None], wk, nt, "mla_k", LANES, 2 * LANES,
                         rope_tabs=rope_tabs, kr_block=(qr + kvr) // LANES)
            vm = _rms_mm(hd, 1, mla_kv_norm[j][None], wv, nt, "plain", 512, 512)
            o = _attention(qm, 0, km, 0, vm, 0, 2 * LANES, LANES, heads, batch, seq, ctx_len, 1)
            xs_new = _mm_postnorm(o, mla_w_out[j].astype(BF16), xs, mods4, affine4, i, 2, nt, lat_rows)
        else:
            heads = d // (2 * DA_HEAD)
            lam_init = 0.8 - 0.6 * math.exp(-0.3 * i)
            qkv = _lnmod_mm(xs, mods4, i, 0, da_w_in[j].astype(BF16), nt, lat_rows, BF16, tn=512,
                            epilogue="rope", rope_tabs=rope_tabs, n_rope_tiles=2 * d // 512,
                            q_tiles=d // 512, q_scale=DA_HEAD ** -0.5 * LOG2E)
            o = _attention(qkv, 0, qkv, heads, qkv, 2 * heads, LANES, LANES, heads, batch, seq, ctx_len, 2,
                           extra=(da_lambda[j], da_subln[j][None]), lam_init=lam_init)
            xs_new = _mm_postnorm(o, da_w_out[j].astype(BF16), xs, mods4, affine4, i, 2, nt, lat_rows)
        xs = xs_new
        wq_p = peer_wq[i].reshape(d, PEER_HEADS, 2, -1).transpose(0, 2, 1, 3).reshape(d, -1).astype(BF16)
        xs = _peer_ffn(xs, mods4, affine4, i, wq_p, _expand_keys(peer_k1[i]), _expand_keys(peer_k2[i]),
                       peer_u[i].astype(BF16), peer_v[i].astype(BF16), nt, lat_rows)
    return xs[:n_lat].reshape(batch, seq, d)
```

```python
import functools
import math

import jax
import jax.numpy as jnp
from jax import lax
from jax.experimental import pallas as pl
from jax.experimental.pallas import tpu as pltpu

F32 = jnp.float32
BF16 = jnp.bfloat16
I32 = jnp.int32

LN_EPS = 1e-6
DEPTH = 4
ALPHA = (2.0 * DEPTH) ** 0.25
GRID_W = 64
ROPE_DIM = 64
ROPE_BASE = 10000.0
GM_CHUNK = 128
MLA_NOPE = 128
MLA_ROPE = 64
DA_HEAD = 64
PEER_HEADS = 8
PEER_TOPK = 16
PEER_NKEYS = 128

LANES = 128
TM = 512
MIB = 1024 * 1024

NT_DIMS = (((1,), (1,)), ((), ()))
LOG2E = math.log2(math.e)


def _cparams(sem, vmem_mib):
    return pltpu.CompilerParams(dimension_semantics=sem, vmem_limit_bytes=vmem_mib * MIB)


def _ln(xf):
    mu = jnp.mean(xf, axis=-1, keepdims=True)
    var = jnp.mean(jnp.square(xf - mu), axis=-1, keepdims=True)
    return (xf - mu) * lax.rsqrt(var + LN_EPS)


def _rms(xf):
    return xf * lax.rsqrt(jnp.mean(jnp.square(xf), axis=-1, keepdims=True) + LN_EPS)


def _swap16(x):
    n = x.shape[-1]
    lane = lax.broadcasted_iota(I32, x.shape, x.ndim - 1)
    up = pltpu.roll(x, n - 16, x.ndim - 1)
    dn = pltpu.roll(x, 16, x.ndim - 1)
    return jnp.where((lane & 16) == 0, up, dn)


def _rope(x, cos, sin):
    reps = x.shape[-1] // LANES
    if reps > 1:
        cos = jnp.concatenate([cos] * reps, axis=-1)
        sin = jnp.concatenate([sin] * reps, axis=-1)
    return x * cos + _swap16(x) * sin


def _mod_kernel(c_ref, w_ref, b_ref, o_ref):
    c = c_ref[...]
    s = c * jax.nn.sigmoid(c)
    o_ref[...] = jnp.dot(s.astype(BF16), w_ref[...].astype(BF16),
                         preferred_element_type=F32) + b_ref[...]


def _mods(c_all, mod_w, mod_b):
    depth, d, n = mod_w.shape
    tn = 1024
    return pl.pallas_call(
        _mod_kernel,
        grid=(depth, n // tn),
        in_specs=[
            pl.BlockSpec((8, d), lambda l, j: (0, 0)),
            pl.BlockSpec((None, d, tn), lambda l, j: (l, 0, j)),
            pl.BlockSpec((None, 1, tn), lambda l, j: (l, 0, j)),
        ],
        out_specs=pl.BlockSpec((None, 8, tn), lambda l, j: (l, 0, j)),
        out_shape=jax.ShapeDtypeStruct((depth, 8, n), F32),
        compiler_params=_cparams(("parallel", "arbitrary"), 40),
        name="mods",
    )(c_all, mod_w, mod_b.reshape(depth, 1, n))


def _grp(i, tm, lat_rows):
    return jnp.minimum((i * tm) // lat_rows, 2)


def _mod_spec(layer, k, d, tm, lat_rows):
    return pl.BlockSpec((None, None, 1, d), lambda i, *_: (layer, _grp(i, tm, lat_rows), 0, k))


def _affine_spec(layer, which, d):
    return pl.BlockSpec((None, None, 1, d), lambda i, *_: (layer, which, 0, 0))


def _lnmod_mm_kernel(*refs, epilogue, n_rope_tiles, q_tiles, q_scale):
    if epilogue == "rope":
        x_ref, sh_ref, sc_ref, w_ref, cos_ref, sin_ref, o_ref, h_ref = refs
    else:
        x_ref, sh_ref, sc_ref, w_ref, o_ref, h_ref = refs
    j = pl.program_id(1)

    @pl.when(j == 0)
    def _():
        h = _ln(x_ref[...]) * (1.0 + sc_ref[...]) + sh_ref[...]
        h_ref[...] = h.astype(BF16)

    acc = jnp.dot(h_ref[...], w_ref[...], preferred_element_type=F32)
    if epilogue == "gelu":
        o_ref[...] = jax.nn.gelu(acc).astype(o_ref.dtype)
    elif epilogue == "rope":
        @pl.when(j < q_tiles)
        def _():
            o_ref[...] = (_rope(acc, cos_ref[...], sin_ref[...]) * q_scale).astype(o_ref.dtype)

        @pl.when(jnp.logical_and(j >= q_tiles, j < n_rope_tiles))
        def _():
            o_ref[...] = _rope(acc, cos_ref[...], sin_ref[...]).astype(o_ref.dtype)

        @pl.when(j >= n_rope_tiles)
        def _():
            o_ref[...] = acc.astype(o_ref.dtype)
    else:
        o_ref[...] = acc.astype(o_ref.dtype)


def _lnmod_mm(x, mods4, layer, k_shift, w, nt, lat_rows, out_dtype, tn, epilogue="none",
              rope_tabs=None, n_rope_tiles=0, q_tiles=0, q_scale=1.0):
    d = x.shape[1]
    n = w.shape[1]
    tm = TM
    in_specs = [
        pl.BlockSpec((tm, d), lambda i, j: (i, 0)),
        _mod_spec(layer, k_shift, d, tm, lat_rows),
        _mod_spec(layer, k_shift + 1, d, tm, lat_rows),
        pl.BlockSpec((d, tn), lambda i, j: (0, j)),
    ]
    args = [x, mods4, mods4, w]
    if epilogue == "rope":
        in_specs += [pl.BlockSpec((tm, LANES), lambda i, j: (i, 0))] * 2
        args += list(rope_tabs)
    return pl.pallas_call(
        functools.partial(_lnmod_mm_kernel, epilogue=epilogue, n_rope_tiles=n_rope_tiles,
                          q_tiles=q_tiles, q_scale=q_scale),
        grid=(nt, n // tn),
        in_specs=in_specs,
        out_specs=pl.BlockSpec((tm, tn), lambda i, j: (i, j)),
        out_shape=jax.ShapeDtypeStruct((nt * tm, n), out_dtype),
        scratch_shapes=[pltpu.VMEM((tm, d), BF16)],
        compiler_params=_cparams(("parallel", "arbitrary"), 40),
        name="lnmod_mm_" + epilogue,
    )(*args)


def _mm_postnorm_kernel(a_ref, w_ref, x_ref, gate_ref, g_ref, b_ref, o_ref):
    y = jnp.dot(a_ref[...], w_ref[...], preferred_element_type=F32)
    z = ALPHA * x_ref[...] + gate_ref[...] * y
    o_ref[...] = _ln(z) * g_ref[...] + b_ref[...]


def _mm_postnorm(a, w, x, mods4, affine4, layer, k_gate, nt, lat_rows):
    d = x.shape[1]
    kdim = a.shape[1]
    tm = TM
    return pl.pallas_call(
        _mm_postnorm_kernel,
        grid=(nt,),
        in_specs=[
            pl.BlockSpec((tm, kdim), lambda i: (i, 0)),
            pl.BlockSpec((kdim, d), lambda i: (0, 0)),
            pl.BlockSpec((tm, d), lambda i: (i, 0)),
            _mod_spec(layer, k_gate, d, tm, lat_rows),
            _affine_spec(layer, 0, d),
            _affine_spec(layer, 0, d),
        ],
        out_specs=pl.BlockSpec((tm, d), lambda i: (i, 0)),
        out_shape=jax.ShapeDtypeStruct((nt * tm, d), F32),
        compiler_params=_cparams(("parallel",), 48),
        name="mm_postnorm",
    )(a, w, x, mods4, affine4[0], affine4[1])


def _gmlp_out_kernel(u_ref, v_ref, lg_ref, lb_ref, ws_ref, bs_ref, w_ref, x_ref, gate_ref,
                     g_ref, b_ref, o_ref, vn_ref, t_ref):
    tm, gd = u_ref.shape
    groups = gd // LANES
    vn_ref[...] = (_ln(v_ref[...].astype(F32)) * lg_ref[...] + lb_ref[...]).astype(BF16)
    for c in range(tm // GM_CHUNK):
        rows = pl.ds(c * GM_CHUNK, GM_CHUNK)
        for g in range(groups):
            cols = pl.ds(g * LANES, LANES)
            sv = jnp.dot(ws_ref[g], vn_ref[rows, cols], preferred_element_type=F32) + bs_ref[g]
            t_ref[rows, cols] = (u_ref[rows, cols].astype(F32) * sv).astype(BF16)
    y = jnp.dot(t_ref[...], w_ref[...], preferred_element_type=F32)
    z = ALPHA * x_ref[...] + gate_ref[...] * y
    o_ref[...] = _ln(z) * g_ref[...] + b_ref[...]


def _gmlp_out(z, ln_g, ln_b, ws, bsb, w_out, x, mods4, affine4, layer, nt, lat_rows):
    d = x.shape[1]
    gd = z.shape[1] // 2
    groups = gd // LANES
    tm = TM // 2
    nt = nt * 2
    return pl.pallas_call(
        _gmlp_out_kernel,
        grid=(nt,),
        in_specs=[
            pl.BlockSpec((tm, gd), lambda i: (i, 0)),
            pl.BlockSpec((tm, gd), lambda i: (i, 1)),
            pl.BlockSpec((1, gd), lambda i: (0, 0)),
            pl.BlockSpec((1, gd), lambda i: (0, 0)),
            pl.BlockSpec((groups, GM_CHUNK, GM_CHUNK), lambda i: (0, 0, 0)),
            pl.BlockSpec((groups, GM_CHUNK, LANES), lambda i: (0, 0, 0)),
            pl.BlockSpec((gd, d), lambda i: (0, 0)),
            pl.BlockSpec((tm, d), lambda i: (i, 0)),
            _mod_spec(layer, 2, d, tm, lat_rows),
            _affine_spec(layer, 0, d),
            _affine_spec(layer, 0, d),
        ],
        out_specs=pl.BlockSpec((tm, d), lambda i: (i, 0)),
        out_shape=jax.ShapeDtypeStruct((nt * tm, d), F32),
        scratch_shapes=[pltpu.VMEM((tm, gd), BF16), pltpu.VMEM((tm, gd), BF16)],
        compiler_params=_cparams(("parallel",), 56),
        name="gmlp_out",
    )(z, z, ln_g, ln_b, ws, bsb, w_out, x, mods4, affine4[0], affine4[1])


def _rms_mm_kernel(*refs, mode, q_scale):
    if mode == "mla_q":
        c_ref, nw_ref, w_ref, cos_ref, sin_ref, o_ref, cn_ref = refs
    elif mode == "mla_k":
        c_ref, nw_ref, w_ref, kr_ref, cos_ref, sin_ref, o_ref, cn_ref = refs
    else:
        c_ref, nw_ref, w_ref, o_ref, cn_ref = refs
    j = pl.program_id(1)

    @pl.when(j == 0)
    def _():
        cn_ref[...] = (_rms(c_ref[...]) * nw_ref[...]).astype(BF16)

    acc = jnp.dot(cn_ref[...], w_ref[...], preferred_element_type=F32)
    if mode == "mla_q":
        roped = _rope(acc[:, LANES:], cos_ref[...], sin_ref[...])
        o_ref[...] = (jnp.concatenate([acc[:, :LANES], roped], axis=-1) * q_scale).astype(o_ref.dtype)
    elif mode == "mla_k":
        roped = _rope(kr_ref[...], cos_ref[...], sin_ref[...])
        o_ref[...] = jnp.concatenate([acc, roped], axis=-1).astype(o_ref.dtype)
    else:
        o_ref[...] = acc.astype(o_ref.dtype)


def _rms_mm(hd, col_block, norm_w, w, nt, mode, tn_w, tn_o, rope_tabs=None, kr_block=None, q_scale=1.0):
    rank = norm_w.shape[1]
    n_w = w.shape[1]
    tm = TM
    steps = n_w // tn_w
    in_specs = [
        pl.BlockSpec((tm, rank), lambda i, j: (i, col_block)),
        pl.BlockSpec((1, rank), lambda i, j: (0, 0)),
        pl.BlockSpec((rank, tn_w), lambda i, j: (0, j)),
    ]
    args = [hd, norm_w, w]
    if mode == "mla_k":
        in_specs.append(pl.BlockSpec((tm, LANES), lambda i, j: (i, kr_block)))
        args.append(hd)
    if mode in ("mla_q", "mla_k"):
        in_specs += [pl.BlockSpec((tm, LANES), lambda i, j: (i, 0))] * 2
        args += list(rope_tabs)
    return pl.pallas_call(
        functools.partial(_rms_mm_kernel, mode=mode, q_scale=q_scale),
        grid=(nt, steps),
        in_specs=in_specs,
        out_specs=pl.BlockSpec((tm, tn_o), lambda i, j: (i, j)),
        out_shape=jax.ShapeDtypeStruct((nt * tm, steps * tn_o), BF16),
        scratch_shapes=[pltpu.VMEM((tm, rank), BF16)],
        compiler_params=_cparams(("parallel", "arbitrary"), 32),
        name="rms_mm_" + mode,
    )(*args)


def _attn_kernel(*refs, n_streams, has_lat, ck, lam_init):
    refs = list(refs)
    o_ref = refs.pop()
    q_ref = refs.pop(0)
    if has_lat:
        kl_ref, kc_ref, vl_ref, vc_ref = refs[:4]
        refs = refs[4:]
    else:
        kc_ref, vc_ref = refs[:2]
        refs = refs[2:]
    tq, dk = q_ref.shape
    dv = vc_ref.shape[1]
    q = q_ref[...]
    if n_streams == 2:
        lane = lax.broadcasted_iota(I32, (tq, dk), 1)
        zero = jnp.zeros_like(q)
        q = jnp.concatenate([jnp.where(lane < dk // 2, q, zero), jnp.where(lane >= dk // 2, q, zero)], axis=0)

    def chunk(carry, k, v):
        s = lax.dot_general(q, k, NT_DIMS, preferred_element_type=F32)
        stats, es = [], []
        for t in range(n_streams):
            m, l, _ = carry[3 * t:3 * t + 3]
            st = s[t * tq:(t + 1) * tq]
            m_new = jnp.maximum(m, jnp.max(st, axis=-1, keepdims=True))
            alpha = jnp.exp2(m - m_new)
            e = jnp.exp2(st - m_new)
            stats.append((m_new, alpha * l + jnp.sum(e, axis=-1, keepdims=True), alpha))
            es.append(e.astype(BF16))
        pv = jnp.dot(es[0] if n_streams == 1 else jnp.concatenate(es, axis=0), v, preferred_element_type=F32)
        out = []
        for t, (m_new, l, alpha) in enumerate(stats):
            out += [m_new, l, alpha * carry[3 * t + 2] + pv[t * tq:(t + 1) * tq]]
        return tuple(out)

    init = []
    for _ in range(n_streams):
        init += [jnp.full((tq, 1), -jnp.inf, F32), jnp.zeros((tq, 1), F32), jnp.zeros((tq, dv), F32)]
    carry = chunk(tuple(init), kc_ref[...], vc_ref[...])
    if has_lat:
        for c in range(kl_ref.shape[0] // ck):
            carry = chunk(carry, kl_ref[c * ck:(c + 1) * ck, :], vl_ref[c * ck:(c + 1) * ck, :])

    if n_streams == 1:
        _, l, acc = carry
        o_ref[...] = (acc / l).astype(o_ref.dtype)
    else:
        lam_ref, sub_ref = refs
        _, l1, a1, _, l2, a2 = carry
        lp = lam_ref[...].astype(F32)
        lam = (jnp.exp(jnp.sum(lp[0:1] * lp[1:2], axis=-1, keepdims=True))
               - jnp.exp(jnp.sum(lp[2:3] * lp[3:4], axis=-1, keepdims=True)) + lam_init)
        o = a1 / l1 - lam * (a2 / l2)
        o_ref[...] = (_rms(o) * sub_ref[...] * (1.0 - lam_init)).astype(o_ref.dtype)


def _attention(q_arr, q_cb, k_arr, k_cb, v_arr, v_cb, dk, dv, heads, batch, lat_len, ctx_len,
               n_streams, extra=(), lam_init=0.0):
    ctx_block0 = batch * lat_len // ctx_len
    outs = []
    for has_lat in (True, False):
        tq = (1024 if n_streams == 1 else 512) if has_lat else ctx_len
        n_q = lat_len // tq if has_lat else 1
        q_block0 = 0 if has_lat else ctx_block0

        def q_map(b, h, qi, n_q=n_q, q_block0=q_block0):
            return (q_block0 + b * n_q + qi, q_cb + h)

        kc_spec = pl.BlockSpec((ctx_len, dk), lambda b, h, qi: (ctx_block0 + b, k_cb + h))
        vc_spec = pl.BlockSpec((ctx_len, dv), lambda b, h, qi: (ctx_block0 + b, v_cb + h))
        in_specs = [pl.BlockSpec((tq, dk), q_map)]
        args = [q_arr]
        if has_lat:
            in_specs += [pl.BlockSpec((lat_len, dk), lambda b, h, qi: (b, k_cb + h)), kc_spec,
                         pl.BlockSpec((lat_len, dv), lambda b, h, qi: (b, v_cb + h)), vc_spec]
            args += [k_arr, k_arr, v_arr, v_arr]
        else:
            in_specs += [kc_spec, vc_spec]
            args += [k_arr, v_arr]
        for e in extra:
            in_specs.append(pl.BlockSpec(e.shape, lambda b, h, qi: (0, 0)))
            args.append(e)
        outs.append(pl.pallas_call(
            functools.partial(_attn_kernel, n_streams=n_streams, has_lat=has_lat, ck=1024, lam_init=lam_init),
            grid=(batch, heads, n_q),
            in_specs=in_specs,
            out_specs=pl.BlockSpec((tq, dv), lambda b, h, qi, n_q=n_q: (b * n_q + qi, h)),
            out_shape=jax.ShapeDtypeStruct((batch * n_q * tq, heads * dv), BF16),
            compiler_params=_cparams(("parallel", "parallel", "arbitrary"), 56),
            name="attention_%d_%s" % (n_streams, "lat" if has_lat else "ctx"),
        )(*args))
    return jnp.concatenate(outs, axis=0)


_PAIRS = [(r1, r2) for r1 in range(PEER_TOPK) for r2 in range(PEER_TOPK)
          if (r1 + 1) * (r2 + 1) <= PEER_TOPK]


def _peer_topk_kernel(q_ref, k1_ref, k2_ref, a_ref, b_ref, g_ref, s_ref, v_ref, i_ref):
    tt = q_ref.shape[0]
    half = q_ref.shape[1] // 2
    nk = PEER_NKEYS
    n_chain = 4

    for side, k_ref in enumerate((k1_ref, k2_ref)):
        qs = q_ref[:, side * half:(side + 1) * half]
        s = lax.dot_general(k_ref[...], qs, NT_DIMS, preferred_element_type=F32)
        s_ref[...] = s.reshape(nk, PEER_HEADS, tt)

        def extract(r, prev):
            ms = [jnp.full(prev.shape, -jnp.inf, F32) for _ in range(n_chain)]
            ids = [jnp.zeros(prev.shape, I32) for _ in range(n_chain)]
            for n in range(nk):
                sn = jnp.where(prev == n, -jnp.inf, s_ref[n])
                s_ref[n] = sn
                c = n % n_chain
                gt = sn > ms[c]
                ms[c] = jnp.where(gt, sn, ms[c])
                ids[c] = jnp.where(gt, n, ids[c])
            while len(ms) > 1:
                m_a, m_b, i_a, i_b = ms[0], ms[1], ids[0], ids[1]
                take_b = jnp.logical_or(m_b > m_a, jnp.logical_and(m_b == m_a, i_b < i_a))
                ms = ms[2:] + [jnp.where(take_b, m_b, m_a)]
                ids = ids[2:] + [jnp.where(take_b, i_b, i_a)]
            v_ref[side, r] = ms[0]
            i_ref[side, r] = ids[0]
            return ids[0]

        lax.fori_loop(0, PEER_TOPK, extract, jnp.full((PEER_HEADS, tt), -1, I32))

    v1 = [v_ref[0, r] for r in range(PEER_TOPK)]
    v2 = [v_ref[1, r] for r in range(PEER_TOPK)]
    i1 = [i_ref[0, r] for r in range(PEER_TOPK)]
    i2 = [i_ref[1, r] for r in range(PEER_TOPK)]
    cand = [v1[r1] + v2[r2] for r1, r2 in _PAIRS]
    flat = [r1 * PEER_TOPK + r2 for r1, r2 in _PAIRS]
    best, a_sel, b_sel = [], [], []
    for _ in range(PEER_TOPK):
        m = functools.reduce(jnp.maximum, cand)
        fsel = functools.reduce(jnp.minimum,
                                [jnp.where(cv == m, f, PEER_TOPK * PEER_TOPK) for cv, f in zip(cand, flat)])
        a = jnp.zeros_like(i1[0])
        b = jnp.zeros_like(i1[0])
        for c, ((r1, r2), f) in enumerate(zip(_PAIRS, flat)):
            hit = fsel == f
            a = jnp.where(hit, i1[r1], a)
            b = jnp.where(hit, i2[r2], b)
            cand[c] = jnp.where(hit, -jnp.inf, cand[c])
        best.append(m)
        a_sel.append(a)
        b_sel.append(b)
    ex = [jnp.exp(v - best[0]) for v in best]
    denom = functools.reduce(jnp.add, ex)
    gates = [e / denom for e in ex]
    nj = PEER_TOPK * PEER_HEADS
    a_ref[...] = jnp.stack(a_sel, axis=0).reshape(nj, tt).T
    b_ref[...] = jnp.stack(b_sel, axis=0).reshape(nj, tt).T
    g_ref[...] = jnp.stack(gates, axis=0).reshape(nj, tt).T


def _peer_topk(q, k1big, k2big, nt_rows):
    tt = 2 * LANES
    nj = PEER_TOPK * PEER_HEADS
    spec = pl.BlockSpec((tt, nj), lambda i: (i, 0))
    return pl.pallas_call(
        _peer_topk_kernel,
        grid=(nt_rows // tt,),
        in_specs=[
            pl.BlockSpec((tt, q.shape[1]), lambda i: (i, 0)),
            pl.BlockSpec(k1big.shape, lambda i: (0, 0)),
            pl.BlockSpec(k2big.shape, lambda i: (0, 0)),
        ],
        out_specs=[spec, spec, spec],
        out_shape=[jax.ShapeDtypeStruct((nt_rows, nj), I32), jax.ShapeDtypeStruct((nt_rows, nj), I32),
                   jax.ShapeDtypeStruct((nt_rows, nj), F32)],
        scratch_shapes=[pltpu.VMEM((PEER_NKEYS, PEER_HEADS, tt), F32),
                        pltpu.VMEM((2, PEER_TOPK, PEER_HEADS, tt), F32),
                        pltpu.VMEM((2, PEER_TOPK, PEER_HEADS, tt), I32)],
        compiler_params=_cparams(("parallel",), 32),
        name="peer_topk",
    )(q, k1big, k2big)


def _peer_gates_kernel(a_ref, b_ref, g_ref, o_ref, s_ref):
    tt, nj = a_ref.shape
    nk = PEER_NKEYS
    key = lax.broadcasted_iota(I32, (nk, nj), 0)

    def token(t, _):
        row = pl.ds(t, 1)
        left = jnp.where(key == a_ref[row, :], g_ref[row, :], 0.0).astype(BF16)
        right = jnp.where(key == b_ref[row, :], 1.0, 0.0).astype(BF16)
        s_ref[pl.ds(pl.multiple_of(t * nk, nk), nk), :] = lax.dot_general(
            left, right, NT_DIMS, preferred_element_type=F32)
        return 0

    lax.fori_loop(0, tt, token, 0, unroll=16)
    for i1 in range(nk):
        o_ref[:, i1 * nk:(i1 + 1) * nk] = s_ref[pl.ds(i1, tt, stride=nk), :].astype(o_ref.dtype)


def _peer_gates(a, b, g):
    rows, nj = a.shape
    tt = 128
    nk = PEER_NKEYS
    spec = pl.BlockSpec((tt, nj), lambda i: (i, 0))
    return pl.pallas_call(
        _peer_gates_kernel,
        grid=(rows // tt,),
        in_specs=[spec, spec, spec],
        out_specs=pl.BlockSpec((tt, nk * nk), lambda i: (i, 0)),
        out_shape=jax.ShapeDtypeStruct((rows, nk * nk), BF16),
        scratch_shapes=[pltpu.VMEM((tt * nk, nk), F32)],
        compiler_params=_cparams(("parallel",), 32),
        name="peer_gates",
    )(a, b, g)


def _peer_main_kernel(x_ref, sh_ref, sc_ref, gate_ref, lg_ref, lb_ref, u_ref, v_ref, gm_ref, o_ref,
                      h_ref, acc_ref):
    e = pl.program_id(1)

    @pl.when(e == 0)
    def _():
        h = _ln(x_ref[...]) * (1.0 + sc_ref[...]) + sh_ref[...]
        h_ref[...] = h.astype(BF16)
        acc_ref[...] = jnp.zeros_like(acc_ref)

    a = lax.dot_general(h_ref[...], u_ref[...], NT_DIMS, preferred_element_type=F32)
    w = jax.nn.gelu(a.astype(BF16)) * gm_ref[...]
    acc_ref[...] += jnp.dot(w, v_ref[...], preferred_element_type=F32)

    @pl.when(e == pl.num_programs(1) - 1)
    def _():
        z = ALPHA * x_ref[...] + gate_ref[...] * acc_ref[...]
        o_ref[...] = _ln(z) * lg_ref[...] + lb_ref[...]


def _peer_main(x, mods4, affine4, layer, u_tab, v_tab, gmat, nt, lat_rows):
    d = x.shape[1]
    n_exp = u_tab.shape[0]
    tm = TM
    te = 1024
    return pl.pallas_call(
        _peer_main_kernel,
        grid=(nt, n_exp // te),
        in_specs=[
            pl.BlockSpec((tm, d), lambda i, e: (i, 0)),
            _mod_spec(layer, 3, d, tm, lat_rows),
            _mod_spec(layer, 4, d, tm, lat_rows),
            _mod_spec(layer, 5, d, tm, lat_rows),
            _affine_spec(layer, 1, d),
            _affine_spec(layer, 1, d),
            pl.BlockSpec((te, d), lambda i, e: (e, 0)),
            pl.BlockSpec((te, d), lambda i, e: (e, 0)),
            pl.BlockSpec((tm, te), lambda i, e: (i, e)),
        ],
        out_specs=pl.BlockSpec((tm, d), lambda i, e: (i, 0)),
        out_shape=jax.ShapeDtypeStruct((nt * tm, d), F32),
        scratch_shapes=[pltpu.VMEM((tm, d), BF16), pltpu.VMEM((tm, d), F32)],
        compiler_params=_cparams(("parallel", "arbitrary"), 56),
        name="peer_main",
    )(x, mods4, mods4, mods4, affine4[0], affine4[1], u_tab, v_tab, gmat)


def _peer_ffn(x, mods4, affine4, layer, wq, k1big, k2big, u_tab, v_tab, nt, lat_rows):
    q = _lnmod_mm(x, mods4, layer, 3, wq, nt, lat_rows, BF16, tn=512)
    a, b, g = _peer_topk(q, k1big, k2big, nt * TM)
    gmat = _peer_gates(a, b, g)
    return _peer_main(x, mods4, affine4, layer, u_tab, v_tab, gmat, nt, lat_rows)


def _rope_tables(lat_len, batch, ctx_rows):
    rows = lat_len // GRID_W
    row = jnp.repeat(jnp.arange(rows, dtype=F32), GRID_W)
    col = jnp.tile(jnp.arange(GRID_W, dtype=F32), rows)
    nf = ROPE_DIM // 4
    inv = ROPE_BASE ** (-jnp.arange(nf, dtype=F32) / nf)
    ang_r = row[:, None] * inv
    ang_c = col[:, None] * inv
    cr, sr, cc, sc = jnp.cos(ang_r), jnp.sin(ang_r), jnp.cos(ang_c), jnp.sin(ang_c)
    cos64 = jnp.concatenate([cr, cr, cc, cc], axis=-1)
    sin64 = jnp.concatenate([-sr, sr, -sc, sc], axis=-1)
    cos = jnp.tile(jnp.concatenate([cos64, cos64], axis=-1), (batch, 1))
    sin = jnp.tile(jnp.concatenate([sin64, sin64], axis=-1), (batch, 1))
    cos = jnp.concatenate([cos, jnp.ones((ctx_rows, LANES), F32)], axis=0)
    sin = jnp.concatenate([sin, jnp.zeros((ctx_rows, LANES), F32)], axis=0)
    return cos, sin


def _expand_keys(k):
    nk, dk = k.shape
    eye = jnp.eye(PEER_HEADS, dtype=k.dtype)
    return jnp.einsum("nd,hg->nhgd", k, eye).reshape(nk * PEER_HEADS, PEER_HEADS * dk).astype(BF16)


def kernel(x, c, ctx, c_ctx, mod_w, mod_b, ln_g, ln_b, peer_wq, peer_k1, peer_k2, peer_u, peer_v,
           gm_w_in, gm_ln_g, gm_ln_b, gm_ws, gm_bs, gm_w_out,
           mla_w_in, mla_q_norm, mla_kv_norm, mla_w_uq, mla_w_ukv, mla_w_out,
           da_w_in, da_lambda, da_subln, da_w_out):
    batch, seq, d = x.shape
    ctx_len = ctx.shape[1]
    lat_rows = seq
    n_lat = batch * seq
    n_all = n_lat + batch * ctx_len
    nt_lat = n_lat // TM
    nt_all = n_all // TM

    xs = jnp.concatenate([x.reshape(n_lat, d), ctx.reshape(batch * ctx_len, d)], axis=0)
    c_all = jnp.concatenate([c, c_ctx[None], jnp.zeros((8 - batch - 1, d), F32)], axis=0)
    mods4 = _mods(c_all, mod_w, mod_b).reshape(DEPTH, 8, 1, 6 * d)
    rope_tabs = _rope_tables(seq, batch, batch * ctx_len)

    for i in range(DEPTH):
        kind = i % 3
        j = i // 3
        last = i == DEPTH - 1
        nt = nt_lat if last else nt_all
        affine4 = (ln_g.reshape(DEPTH, 2, 1, d), ln_b.reshape(DEPTH, 2, 1, d))
        if kind == 0:
            z = _lnmod_mm(xs, mods4, i, 0, gm_w_in[j].astype(BF16), nt, lat_rows, F32, tn=512,
                          epilogue="gelu")
            groups = gm_ws.shape[1]
            bsb = jnp.broadcast_to(gm_bs[j][:, :, None], (groups, GM_CHUNK, LANES))
            xs_new = _gmlp_out(z, gm_ln_g[j][None], gm_ln_b[j][None], gm_ws[j].astype(BF16), bsb,
                               gm_w_out[j].astype(BF16), xs, mods4, affine4, i, nt, lat_rows)
        elif kind == 1:
            heads = mla_w_out.shape[1] // LANES
            qr = mla_q_norm.shape[1]
            kvr = mla_kv_norm.shape[1]
            w_in = jnp.pad(mla_w_in[j], ((0, 0), (0, LANES - MLA_ROPE))).astype(BF16)
            hd = _lnmod_mm(xs, mods4, i, 0, w_in, nt, lat_rows, F32, tn=w_in.shape[1] // 3)
            wq = mla_w_uq[j].reshape(qr, heads, MLA_NOPE + MLA_ROPE)
            wq = jnp.pad(wq, ((0, 0), (0, 0), (0, 2 * LANES - MLA_NOPE - MLA_ROPE)))
            wq = wq.reshape(qr, heads * 2 * LANES).astype(BF16)
            wkv = mla_w_ukv[j].reshape(kvr, heads, 2 * LANES)
            wk = wkv[:, :, :MLA_NOPE].reshape(kvr, heads * LANES).astype(BF16)
            wv = wkv[:, :, MLA_NOPE:].reshape(kvr, heads * LANES).astype(BF16)
            scale = (MLA_NOPE + MLA_ROPE) ** -0.5 * LOG2E
            qm = _rms_mm(hd, 0, mla_q_norm[j][None], wq, nt, "mla_q", 2 * LANES, 2 * LANES,
                         rope_tabs=rope_tabs, q_scale=scale)
            km = _rms_mm(hd, 1, mla_kv_norm[j][None], wk, nt, "mla_k", LANES, 2 * LANES,
                         rope_tabs=rope_tabs, kr_block=(qr + kvr) // LANES)
            vm = _rms_mm(hd, 1, mla_kv_norm[j][None], wv, nt, "plain", 512, 512)
            o = _attention(qm, 0, km, 0, vm, 0, 2 * LANES, LANES, heads, batch, seq, ctx_len, 1)
            xs_new = _mm_postnorm(o, mla_w_out[j].astype(BF16), xs, mods4, affine4, i, 2, nt, lat_rows)
        else:
            heads = d // (2 * DA_HEAD)
            lam_init = 0.8 - 0.6 * math.exp(-0.3 * i)
            qkv = _lnmod_mm(xs, mods4, i, 0, da_w_in[j].astype(BF16), nt, lat_rows, BF16, tn=512,
                            epilogue="rope", rope_tabs=rope_tabs, n_rope_tiles=2 * d // 512,
                            q_tiles=d // 512, q_scale=DA_HEAD ** -0.5 * LOG2E)
            o = _attention(qkv, 0, qkv, heads, qkv, 2 * heads, LANES, LANES, heads, batch, seq, ctx_len, 2,
                           extra=(da_lambda[j], da_subln[j][None]), lam_init=lam_init)
            xs_new = _mm_postnorm(o, da_w_out[j].astype(BF16), xs, mods4, affine4, i, 2, nt, lat_rows)
        xs = xs_new
        wq_p = peer_wq[i].reshape(d, PEER_HEADS, 2, -1).transpose(0, 2, 1, 3).reshape(d, -1).astype(BF16)
        xs = _peer_ffn(xs, mods4, affine4, i, wq_p, _expand_keys(peer_k1[i]), _expand_keys(peer_k2[i]),
                       peer_u[i].astype(BF16), peer_v[i].astype(BF16), nt, lat_rows)
    return xs[:n_lat].reshape(batch, seq, d)
```
